```python
import jax, jax.numpy as jnp
from jax import lax
import numpy as np

D_MODEL = 2048
BATCH = 4
SEQ = 4096
DEPTH = 2

GRID_W = 64
CTX_LEN = 256
HEAD_DIM_A = 128
ATTN_WIDTH = D_MODEL // 2
N_HEADS_A = ATTN_WIDTH // HEAD_DIM_A
WIN_H = 8
WIN_W = 16
CONV_CH = D_MODEL // 2
CONV_K = 3
MIX_WIDTH = ATTN_WIDTH + CONV_CH
MIX_IN_WIDTH = 3 * ATTN_WIDTH + 3 * CONV_CH
N_FOURIER_GROUPS = 4
N_EXPERTS = 32
N_EXPERT_GROUPS = 8
EXPERTS_PER_GROUP = N_EXPERTS // N_EXPERT_GROUPS
TOP_K = 2
D_EXPERT = D_MODEL // 2
EXPERT_BLOCK = 256
ALPHA = (2 * DEPTH) ** 0.25
BETA = (8 * DEPTH) ** -0.25
LN_EPS = 1e-6
NEG_INF = -1e30

kernel_name = "hybrid_natten_shortconv_fnet_grouped_moe_dit"


def layer_norm(x, g=None, b=None):
    xf = x.astype(jnp.float32)
    mu = jnp.mean(xf, axis=-1, keepdims=True)
    var = jnp.mean(jnp.square(xf - mu), axis=-1, keepdims=True)
    y = (xf - mu) * lax.rsqrt(var + LN_EPS)
    if g is not None:
        y = y * g.astype(jnp.float32) + b.astype(jnp.float32)
    return y.astype(x.dtype)


def modulate(h, shift, scale):
    return h * (1 + scale) + shift


def split_heads(t):
    return t.reshape(t.shape[0], t.shape[1], N_HEADS_A, HEAD_DIM_A)


def neighbourhood_attention(q, k, v, k_ctx, v_ctx, rpb):
    bsz, length, heads, dh = q.shape
    rows = length // GRID_W
    kh = min(WIN_H, rows)
    scale = dh ** -0.5
    qg = q.reshape(bsz, rows, GRID_W, heads, dh)
    kg = k.reshape(bsz, rows, GRID_W, heads, dh)
    vg = v.reshape(bsz, rows, GRID_W, heads, dh)
    row_start = jnp.clip(jnp.arange(rows) - kh // 2, 0, rows - kh)
    cols = jnp.arange(GRID_W)
    col_start = jnp.clip(cols - WIN_W // 2, 0, GRID_W - WIN_W)
    col_mask = (cols[None, :] >= col_start[:, None]) & (cols[None, :] < col_start[:, None] + WIN_W)
    col_bias_idx = jnp.clip(cols[None, :] - cols[:, None], -(WIN_W - 1), WIN_W - 1) + WIN_W - 1

    def row_block(r):
        rs = row_start[r]
        q_r = lax.dynamic_index_in_dim(qg, r, axis=1, keepdims=False)
        k_b = lax.dynamic_slice_in_dim(kg, rs, kh, axis=1)
        v_b = lax.dynamic_slice_in_dim(vg, rs, kh, axis=1)
        row_bias_idx = rs + jnp.arange(kh) - r + WIN_H - 1
        bias = rpb[:, row_bias_idx[:, None, None], col_bias_idx[None, :, :]]
        bias = jnp.transpose(bias, (0, 2, 1, 3)).astype(jnp.float32)
        s_loc = jnp.einsum('bqhd,bakhd->bhqak', q_r, k_b, preferred_element_type=jnp.float32) * scale + bias
        s_loc = jnp.where(col_mask[:, None, :], s_loc, NEG_INF)
        s_ctx = jnp.einsum('bqhd,bchd->bhqc', q_r, k_ctx, preferred_element_type=jnp.float32) * scale
        s = jnp.concatenate([s_loc.reshape(bsz, heads, GRID_W, kh * GRID_W), s_ctx], axis=-1)
        p = jax.nn.softmax(s, axis=-1).astype(v.dtype)
        p_loc = p[..., :kh * GRID_W].reshape(bsz, heads, GRID_W, kh, GRID_W)
        p_ctx = p[..., kh * GRID_W:]
        return (jnp.einsum('bhqak,bakhd->bqhd', p_loc, v_b)
                + jnp.einsum('bhqc,bchd->bqhd', p_ctx, v_ctx))

    out = lax.map(row_block, jnp.arange(rows))
    return jnp.moveaxis(out, 0, 1).reshape(bsz, length, heads * dh)


def context_attention(q, k, v):
    s = jnp.einsum('bqhd,bkhd->bhqk', q, k, preferred_element_type=jnp.float32) * (q.shape[-1] ** -0.5)
    p = jax.nn.softmax(s, axis=-1).astype(v.dtype)
    o = jnp.einsum('bhqk,bkhd->bqhd', p, v)
    return o.reshape(o.shape[0], o.shape[1], -1)


def depthwise_conv3(z, w):
    return lax.conv_general_dilated(
        z, w[:, None, :].astype(z.dtype), window_strides=(1,), padding=((1, 1),),
        dimension_numbers=('NWC', 'WIO', 'NWC'), feature_group_count=z.shape[-1])


def na_conv_mixer(u, uc, w_in, rpb, conv_w, w_out, ctx_update):
    a, ch = ATTN_WIDTH, CONV_CH
    cuts = [a, 2 * a, 3 * a, 3 * a + ch, 3 * a + 2 * ch]
    q, k, v, gb, gc, xin = jnp.split(u @ w_in, cuts, axis=-1)
    if ctx_update:
        qc, kc, vc, gbc, gcc, xinc = jnp.split(uc @ w_in, cuts, axis=-1)
    else:
        kc, vc = jnp.split(uc @ w_in[:, a:3 * a], 2, axis=-1)
    attn = neighbourhood_attention(split_heads(q), split_heads(k), split_heads(v),
                                   split_heads(kc), split_heads(vc), rpb)
    conv = gb * depthwise_conv3(gc * xin, conv_w)
    y = jnp.concatenate([attn, conv], axis=-1) @ w_out
    if not ctx_update:
        return y, None
    attn_c = context_attention(split_heads(qc), split_heads(kc), split_heads(vc))
    conv_c = gbc * depthwise_conv3(gcc * xinc, conv_w)
    y_c = jnp.concatenate([attn_c, conv_c], axis=-1) @ w_out
    return y, y_c


def fourier_mix(u):
    bsz, length, d = u.shape
    z = u.astype(jnp.float32).reshape(bsz, length, N_FOURIER_GROUPS, d // N_FOURIER_GROUPS)
    f = jnp.fft.fft2(z, axes=(1, 3), norm='ortho').real
    return f.reshape(bsz, length, d).astype(u.dtype)


def moe_ffn(h, router_w, router_b, w_gate, w_up, w_down):
    n_tok, d = h.shape
    n_rows = n_tok * TOP_K
    n_blocks = -(-n_rows // EXPERT_BLOCK) + N_EXPERTS
    n_pad = n_blocks * EXPERT_BLOCK
    logits = jnp.einsum('td,de->te', h, router_w, preferred_element_type=jnp.float32)
    affinity = jax.nn.sigmoid(logits)
    biased = (affinity + router_b.astype(jnp.float32)).reshape(n_tok, N_EXPERT_GROUPS, EXPERTS_PER_GROUP)
    group_score = lax.top_k(biased, TOP_K)[0].sum(-1)
    group = jnp.argmax(group_score, axis=-1)
    in_group = jnp.take_along_axis(biased, group[:, None, None], axis=1)[:, 0]
    local = lax.top_k(in_group, TOP_K)[1]
    expert = group[:, None] * EXPERTS_PER_GROUP + local
    weight = jnp.take_along_axis(affinity, expert, axis=1)
    weight = weight / jnp.sum(weight, axis=-1, keepdims=True)
    flat_e = expert.reshape(-1)
    flat_t = jnp.repeat(jnp.arange(n_tok), TOP_K)
    flat_w = weight.reshape(-1)
    order = jnp.argsort(flat_e)
    e_s, t_s, w_s = flat_e[order], flat_t[order], flat_w[order]
    counts = jnp.bincount(flat_e, length=N_EXPERTS)
    padded = (counts + EXPERT_BLOCK - 1) // EXPERT_BLOCK * EXPERT_BLOCK
    start = jnp.cumsum(counts) - counts
    pad_end = jnp.cumsum(padded)
    pad_start = pad_end - padded
    dest = pad_start[e_s] + jnp.arange(n_rows) - start[e_s]
    row_tok = jnp.full((n_pad,), n_tok, jnp.int32).at[dest].set(t_s)
    row_w = jnp.zeros((n_pad,), jnp.float32).at[dest].set(w_s)
    block_expert = jnp.clip(jnp.searchsorted(pad_end, jnp.arange(n_blocks) * EXPERT_BLOCK, side='right'),
                            0, N_EXPERTS - 1)
    h_pad = jnp.concatenate([h, jnp.zeros((1, d), h.dtype)], axis=0)
    xb = h_pad[row_tok].reshape(n_blocks, EXPERT_BLOCK, d)

    def expert_block(args):
        xe, e = args
        return (jax.nn.silu(xe @ w_gate[e]) * (xe @ w_up[e])) @ w_down[e]

    yb = lax.map(expert_block, (xb, block_expert)).reshape(n_pad, d)
    out = jnp.zeros((n_tok + 1, d), jnp.float32).at[row_tok].add(yb.astype(jnp.float32) * row_w[:, None])
    return out[:n_tok].astype(h.dtype)


def setup_inputs(seed: int = 0) -> dict:
    key = jax.random.key(seed)
    ks = jax.random.split(key, 18)
    n_even = (DEPTH + 1) // 2
    n_odd = DEPTH // 2
    nrm = jax.random.normal
    d = D_MODEL
    return {
        'x': nrm(ks[0], (BATCH, SEQ, d), jnp.float32),
        'c': nrm(ks[1], (BATCH, d), jnp.float32),
        'ctx': nrm(ks[2], (BATCH, CTX_LEN, d), jnp.float32),
        'c_ctx': nrm(ks[3], (d,), jnp.float32),
        'ada_w': nrm(ks[4], (DEPTH, d, 6 * d), jnp.float32) * d ** -0.5,
        'ada_b': 0.01 * nrm(ks[5], (DEPTH, 6 * d), jnp.float32),
        'w_mix_in': nrm(ks[6], (n_even, d, MIX_IN_WIDTH), jnp.float32) * d ** -0.5,
        'rpb': 0.02 * nrm(ks[7], (n_even, N_HEADS_A, 2 * WIN_H - 1, 2 * WIN_W - 1), jnp.float32),
        'conv_w': nrm(ks[8], (n_even, CONV_K, CONV_CH), jnp.float32) * CONV_K ** -0.5,
        'w_mix_out': nrm(ks[9], (n_even, MIX_WIDTH, d), jnp.float32) * (MIX_WIDTH ** -0.5 * BETA),
        'w_fourier_out': nrm(ks[10], (n_odd, d, d), jnp.float32) * (d ** -0.5 * BETA),
        'router_w': nrm(ks[11], (d, N_EXPERTS), jnp.float32) * d ** -0.5,
        'router_b': 0.01 * nrm(ks[12], (N_EXPERTS,), jnp.float32),
        'w_gate': nrm(ks[13], (DEPTH, N_EXPERTS, d, D_EXPERT), jnp.float32) * d ** -0.5,
        'w_up': nrm(ks[14], (DEPTH, N_EXPERTS, d, D_EXPERT), jnp.float32) * d ** -0.5,
        'w_down': nrm(ks[15], (DEPTH, N_EXPERTS, D_EXPERT, d), jnp.float32) * (D_EXPERT ** -0.5 * BETA),
        'ln_g': 1.0 + 0.01 * nrm(ks[16], (DEPTH, 2, d), jnp.float32),
        'ln_b': 0.01 * nrm(ks[17], (DEPTH, 2, d), jnp.float32),
    }


def reference(x, c, ctx, c_ctx, ada_w, ada_b, w_mix_in, rpb, conv_w, w_mix_out, w_fourier_out,
              router_w, router_b, w_gate, w_up, w_down, ln_g, ln_b):
    bsz, length, d = x.shape
    xc = ctx
    silu_c = jax.nn.silu(c)
    silu_cc = jax.nn.silu(c_ctx)
    for i in range(DEPTH):
        even = i % 2 == 0
        ctx_update = any(j % 2 == 0 for j in range(i + 1, DEPTH))
        mods = (silu_c @ ada_w[i] + ada_b[i])[:, None, :]
        sh1, sc1, g1, sh2, sc2, g2 = jnp.split(mods, 6, axis=-1)
        u = modulate(layer_norm(x), sh1, sc1)
        uc = None
        if even or ctx_update:
            mods_c = (silu_cc @ ada_w[i] + ada_b[i])[None, None, :]
            csh1, csc1, cg1, csh2, csc2, cg2 = jnp.split(mods_c, 6, axis=-1)
            uc = modulate(layer_norm(xc), csh1, csc1)
        if even:
            y, yc = na_conv_mixer(u, uc, w_mix_in[i // 2], rpb[i // 2], conv_w[i // 2],
                                  w_mix_out[i // 2], ctx_update)
        else:
            y = fourier_mix(u) @ w_fourier_out[i // 2]
            yc = fourier_mix(uc) @ w_fourier_out[i // 2] if ctx_update else None
        x = layer_norm(ALPHA * x + g1 * y, ln_g[i, 0], ln_b[i, 0])
        u2 = modulate(layer_norm(x), sh2, sc2)
        if ctx_update:
            xc = layer_norm(ALPHA * xc + cg1 * yc, ln_g[i, 0], ln_b[i, 0])
            u2c = modulate(layer_norm(xc), csh2, csc2)
            tokens = jnp.concatenate([u2.reshape(-1, d), u2c.reshape(-1, d)], axis=0)
            f = moe_ffn(tokens, router_w, router_b, w_gate[i], w_up[i], w_down[i])
            f_lat = f[:bsz * length].reshape(bsz, length, d)
            f_ctx = f[bsz * length:].reshape(xc.shape)
            xc = layer_norm(ALPHA * xc + cg2 * f_ctx, ln_g[i, 1], ln_b[i, 1])
        else:
            f_lat = moe_ffn(u2.reshape(-1, d), router_w, router_b, w_gate[i], w_up[i],
                            w_down[i]).reshape(bsz, length, d)
        x = layer_norm(ALPHA * x + g2 * f_lat, ln_g[i, 1], ln_b[i, 1])
    return x
```

```python
import functools

import numpy as np
import jax
import jax.numpy as jnp
from jax import lax
from jax.experimental import pallas as pl
from jax.experimental.pallas import tpu as pltpu

D_MODEL = 2048
BATCH = 4
SEQ = 4096
DEPTH = 2
GRID_W = 64
GRID_ROWS = SEQ // GRID_W
CTX_LEN = 256
HEAD_DIM = 128
ATTN_WIDTH = D_MODEL // 2
N_HEADS = ATTN_WIDTH // HEAD_DIM
WIN_H = 8
WIN_W = 16
CONV_CH = D_MODEL // 2
MIX_IN_WIDTH = 3 * ATTN_WIDTH + 3 * CONV_CH
N_FOURIER_GROUPS = 4
FOURIER_CH = D_MODEL // N_FOURIER_GROUPS
N_EXPERTS = 32
N_GROUPS = 8
EXPERTS_PER_GROUP = N_EXPERTS // N_GROUPS
TOP_K = 2
D_EXPERT = D_MODEL // 2
ALPHA = (2 * DEPTH) ** 0.25
LN_EPS = 1e-6
NEG_INF = -1e30
N_TOK = BATCH * SEQ

V7X_VMEM_LIMIT_BYTES = 56 * 1024 * 1024
SUBLANES = 8
LANES = 128

PROJ_TM = 512
PROJ_TN = 1024
ATT_QROWS = 8
ATT_KROWS = 16
ATT_Q = ATT_QROWS * GRID_W
ATT_K = ATT_KROWS * GRID_W
N_QBLOCKS = GRID_ROWS // ATT_QROWS
MIX_TM = 256
DFT_TM = 1024
DFT_TN = 1024
EXPERT_TM = 256
N_ROWS = N_TOK * TOP_K
N_EXPERT_BLOCKS = N_ROWS // EXPERT_TM + N_EXPERTS
N_PAD = N_EXPERT_BLOCKS * EXPERT_TM
DISPATCH_TM = 1024
N_FILL_SEGMENTS = N_EXPERTS + 1
MODS_TN = 1024

F32 = jnp.float32
BF16 = jnp.bfloat16


def _cparams(*sem):
    return pltpu.CompilerParams(dimension_semantics=sem, vmem_limit_bytes=V7X_VMEM_LIMIT_BYTES)


def _ln(x):
    mu = jnp.mean(x, axis=-1, keepdims=True)
    xc = x - mu
    var = jnp.mean(xc * xc, axis=-1, keepdims=True)
    return xc * lax.rsqrt(var + LN_EPS)


def _sigmoid(x):
    return 1.0 / (1.0 + jnp.exp(-x))


def _mods_kernel(c_ref, w_ref, b_ref, o_ref):
    c = c_ref[...]
    s = (c * _sigmoid(c)).astype(BF16)
    o_ref[0] = jnp.dot(s, w_ref[0].astype(BF16), preferred_element_type=F32) + b_ref[0]


def _mods(c8, ada_w, ada_b):
    n = 6 * D_MODEL
    return pl.pallas_call(
        _mods_kernel,
        grid=(DEPTH, n // MODS_TN),
        in_specs=[
            pl.BlockSpec((SUBLANES, D_MODEL), lambda l, j: (0, 0)),
            pl.BlockSpec((1, D_MODEL, MODS_TN), lambda l, j: (l, 0, j)),
            pl.BlockSpec((1, 1, MODS_TN), lambda l, j: (l, 0, j)),
        ],
        out_specs=pl.BlockSpec((1, SUBLANES, MODS_TN), lambda l, j: (l, 0, j)),
        out_shape=jax.ShapeDtypeStruct((DEPTH, SUBLANES, n), F32),
        compiler_params=_cparams("arbitrary", "arbitrary"),
        name="mods",
    )(c8, ada_w, ada_b.reshape(DEPTH, 1, n))


def _ln_proj_kernel(x_ref, m_ref, w_ref, o_ref, u_scr):
    @pl.when(pl.program_id(1) == 0)
    def _():
        m = m_ref[0]
        y = _ln(x_ref[...])
        u_scr[...] = (y * (1.0 + m[:, D_MODEL:2 * D_MODEL]) + m[:, :D_MODEL]).astype(BF16)

    o_ref[...] = jnp.dot(u_scr[...], w_ref[...], preferred_element_type=F32).astype(o_ref.dtype)


def _ln_proj(x2d, mods3, w, col_block0, n_cols, mod_row_of_tile):
    n_rows = x2d.shape[0]
    return pl.pallas_call(
        _ln_proj_kernel,
        grid=(n_rows // PROJ_TM, n_cols // PROJ_TN),
        in_specs=[
            pl.BlockSpec((PROJ_TM, D_MODEL), lambda i, j: (i, 0)),
            pl.BlockSpec((1, 1, 6 * D_MODEL), lambda i, j: (mod_row_of_tile(i), 0, 0)),
            pl.BlockSpec((D_MODEL, PROJ_TN), lambda i, j: (0, col_block0 + j)),
        ],
        out_specs=pl.BlockSpec((PROJ_TM, PROJ_TN), lambda i, j: (i, j)),
        out_shape=jax.ShapeDtypeStruct((n_rows, n_cols), BF16),
        scratch_shapes=[pltpu.VMEM((PROJ_TM, D_MODEL), BF16)],
        compiler_params=_cparams("arbitrary", "arbitrary"),
        name="ln_proj",
    )(x2d, mods3, w)


def _bias_table_kernel(rpb_ref, o_ref):
    h = pl.program_id(0)
    typ = pl.program_id(1)
    a = pl.program_id(2)
    qc = lax.broadcasted_iota(jnp.int32, (GRID_W, GRID_W), 0)
    kc = lax.broadcasted_iota(jnp.int32, (GRID_W, GRID_W), 1)
    col_start = jnp.clip(qc - WIN_W // 2, 0, GRID_W - WIN_W)
    col_ok = (kc >= col_start) & (kc < col_start + WIN_W)
    col_idx = jnp.clip(kc - qc, -(WIN_W - 1), WIN_W - 1) + WIN_W - 1
    blk = jnp.where(typ == 0, 0, jnp.where(typ == 1, 1, N_QBLOCKS - 1))
    r = blk * ATT_QROWS + a
    ks = jnp.clip(blk * ATT_QROWS - WIN_H // 2, 0, GRID_ROWS - ATT_KROWS)
    rs = jnp.clip(r - WIN_H // 2, 0, GRID_ROWS - WIN_H)
    n_rb = 2 * WIN_H - 1
    n_cb = 2 * WIN_W - 1
    for c in range(ATT_KROWS):
        kr = ks + c
        row_ok = (kr >= rs) & (kr < rs + WIN_H)
        ridx = jnp.clip(kr - r + WIN_H - 1, 0, n_rb - 1)
        base = (h * n_rb + ridx) * n_cb
        acc = jnp.zeros((GRID_W, GRID_W), F32)
        for i in range(n_cb):
            acc = jnp.where(col_idx == i, rpb_ref[base + i], acc)
        ok = col_ok & row_ok
        o_ref[0, 0, :, c * GRID_W:(c + 1) * GRID_W] = jnp.where(ok, acc, NEG_INF)


def _bias_table(rpb_l):
    flat = rpb_l.reshape(-1).astype(F32)
    return pl.pallas_call(
        _bias_table_kernel,
        grid=(N_HEADS, 3, ATT_QROWS),
        in_specs=[pl.BlockSpec(memory_space=pltpu.SMEM)],
        out_specs=pl.BlockSpec((1, 1, GRID_W, ATT_K), lambda h, t, a: (h, t, a, 0)),
        out_shape=jax.ShapeDtypeStruct((N_HEADS, 3, ATT_Q, ATT_K), F32),
        compiler_params=_cparams("arbitrary", "arbitrary", "arbitrary"),
        name="bias_table",
    )(flat)


def _natten_kernel(q_ref, k_ref, v_ref, kc_ref, vc_ref, bias_ref, o_ref):
    scale = HEAD_DIM ** -0.5
    kc = kc_ref[...]
    vc = vc_ref[...]
    nt = (((1,), (1,)), ((), ()))

    def body(j, carry):
        typ = jnp.where(j == 0, 0, jnp.where(j == N_QBLOCKS - 1, 2, 1))
        q0 = pl.multiple_of(j * ATT_Q, ATT_Q)
        k0 = pl.multiple_of(
            jnp.clip(j * ATT_QROWS - WIN_H // 2, 0, GRID_ROWS - ATT_KROWS) * GRID_W, 4 * GRID_W)
        q = q_ref[pl.ds(q0, ATT_Q), :]
        kw = k_ref[pl.ds(k0, ATT_K), :]
        vw = v_ref[pl.ds(k0, ATT_K), :]
        s_loc = lax.dot_general(q, kw, nt, preferred_element_type=F32) * scale + bias_ref[typ]
        s_ctx = lax.dot_general(q, kc, nt, preferred_element_type=F32) * scale
        m = jnp.maximum(jnp.max(s_loc, axis=-1, keepdims=True), jnp.max(s_ctx, axis=-1, keepdims=True))
        p_loc = jnp.exp(s_loc - m)
        p_ctx = jnp.exp(s_ctx - m)
        denom = jnp.sum(p_loc, axis=-1, keepdims=True) + jnp.sum(p_ctx, axis=-1, keepdims=True)
        o = (jnp.dot(p_loc.astype(BF16), vw, preferred_element_type=F32)
             + jnp.dot(p_ctx.astype(BF16), vc, preferred_element_type=F32))
        o_ref[pl.ds(q0, ATT_Q), :] = (o / denom).astype(o_ref.dtype)
        return carry

    lax.fori_loop(0, N_QBLOCKS, body, 0)


def _natten(proj, kvc, bias):
    blk = lambda off: pl.BlockSpec((SEQ, HEAD_DIM), lambda b, h: (b, off + h))
    cblk = lambda off: pl.BlockSpec((CTX_LEN, HEAD_DIM), lambda b, h: (b, off + h))
    return pl.pallas_call(
        _natten_kernel,
        grid=(BATCH, N_HEADS),
        in_specs=[blk(0), blk(N_HEADS), blk(2 * N_HEADS), cblk(0), cblk(N_HEADS),
                  pl.BlockSpec((None, 3, ATT_Q, ATT_K), lambda b, h: (h, 0, 0, 0))],
        out_specs=pl.BlockSpec((SEQ, HEAD_DIM), lambda b, h: (b, h)),
        out_shape=jax.ShapeDtypeStruct((N_TOK, ATTN_WIDTH), BF16),
        compiler_params=_cparams("arbitrary", "arbitrary"),
        name="natten",
    )(proj, proj, proj, kvc, kvc, bias)


def _pack_bf16_pair(lo, hi):
    return pltpu.pack_elementwise([lo, hi], packed_dtype=BF16)


def _unpack_bf16_pair(w):
    unpack = lambda i: pltpu.unpack_elementwise(w, index=i, packed_dtype=BF16, unpacked_dtype=F32)
    return unpack(0).astype(BF16), unpack(1).astype(BF16)


def _epilogue(y, x_ref, m_ref, lng_ref, lnb_ref, rwt_ref, rb_ref,
              x1_ref, u2_ref, re_ref, rw_ref, rk_ref, cnt_ref):
    m = m_ref[0]
    g1 = m[:, 2 * D_MODEL:3 * D_MODEL]
    sh2 = m[:, 3 * D_MODEL:4 * D_MODEL]
    sc2 = m[:, 4 * D_MODEL:5 * D_MODEL]
    x1 = _ln(ALPHA * x_ref[...] + g1 * y) * lng_ref[0:1, :] + lnb_ref[0:1, :]
    x1_ref[...] = x1
    u2 = _ln(x1) * (1.0 + sc2) + sh2
    u2_ref[...] = _pack_bf16_pair(u2[:, :D_MODEL // 2], u2[:, D_MODEL // 2:])

    tm = u2.shape[0]
    logits = lax.dot_general(rwt_ref[...], u2, (((1,), (1,)), ((), ())),
                             precision=lax.Precision.HIGHEST, preferred_element_type=F32)
    aff = _sigmoid(logits)
    biased = aff + rb_ref[...]
    a = [biased[l * N_GROUPS:(l + 1) * N_GROUPS, :] for l in range(EXPERTS_PER_GROUP)]
    f = [aff[l * N_GROUPS:(l + 1) * N_GROUPS, :] for l in range(EXPERTS_PER_GROUP)]
    hi01, lo01 = jnp.maximum(a[0], a[1]), jnp.minimum(a[0], a[1])
    hi23, lo23 = jnp.maximum(a[2], a[3]), jnp.minimum(a[2], a[3])
    top1 = jnp.maximum(hi01, hi23)
    top2 = jnp.maximum(jnp.minimum(hi01, hi23), jnp.maximum(lo01, lo23))
    gscore = top1 + top2
    gi = lax.broadcasted_iota(jnp.int32, (N_GROUPS, tm), 0)
    gmax = jnp.max(gscore, axis=0, keepdims=True)
    group = jnp.min(jnp.where(gscore == gmax, gi, N_GROUPS), axis=0, keepdims=True)
    sel = gi == group
    b = [jnp.sum(jnp.where(sel, a[l], 0.0), axis=0, keepdims=True) for l in range(EXPERTS_PER_GROUP)]
    c = [jnp.sum(jnp.where(sel, f[l], 0.0), axis=0, keepdims=True) for l in range(EXPERTS_PER_GROUP)]

    def first_argmax(vals):
        mx = functools.reduce(jnp.maximum, vals)
        idx = jnp.full(mx.shape, len(vals) - 1, jnp.int32)
        for l in range(len(vals) - 2, -1, -1):
            idx = jnp.where(vals[l] == mx, l, idx)
        return idx

    i1 = first_argmax(b)
    i2 = first_argmax([jnp.where(i1 == l, -jnp.inf, b[l]) for l in range(EXPERTS_PER_GROUP)])
    pick = lambda idx: functools.reduce(
        lambda acc, l: jnp.where(idx == l, c[l], acc), range(1, EXPERTS_PER_GROUP), c[0])
    w1, w2 = pick(i1), pick(i2)
    wsum = w1 + w2
    e1 = group * EXPERTS_PER_GROUP + i1
    e2 = group * EXPERTS_PER_GROUP + i2
    re_ref[0:1, :] = e1
    re_ref[1:2, :] = e2
    rw_ref[0:1, :] = w1 / wsum
    rw_ref[1:2, :] = w2 / wsum

    ei = lax.broadcasted_iota(jnp.int32, (N_EXPERTS, tm), 0)
    oh1 = ei == e1
    oh2 = ei == e2
    oh = jnp.where(oh1 | oh2, 1.0, 0.0)
    t_row = lax.broadcasted_iota(jnp.int32, (tm, tm), 0)
    t_col = lax.broadcasted_iota(jnp.int32, (tm, tm), 1)
    before = jnp.where(t_row < t_col, 1.0, 0.0).astype(BF16)
    prefix = jnp.dot(oh.astype(BF16), before, preferred_element_type=F32)
    rk_ref[0:1, :] = jnp.sum(jnp.where(oh1, prefix, 0.0), axis=0, keepdims=True).astype(jnp.int32)
    rk_ref[1:2, :] = jnp.sum(jnp.where(oh2, prefix, 0.0), axis=0, keepdims=True).astype(jnp.int32)
    cnt = jnp.sum(oh, axis=1, keepdims=True)
    cnt_ref[0] = jnp.broadcast_to(cnt, (N_EXPERTS, LANES)).astype(jnp.int32)


def _epilogue_specs(tile_batch):
    n_tiles = N_TOK // MIX_TM
    in_specs = [
        pl.BlockSpec((MIX_TM, D_MODEL), lambda i: (i, 0)),
        pl.BlockSpec((1, 1, 6 * D_MODEL), lambda i: (tile_batch(i), 0, 0)),
        pl.BlockSpec((2, D_MODEL), lambda i: (0, 0)),
        pl.BlockSpec((2, D_MODEL), lambda i: (0, 0)),
        pl.BlockSpec((N_EXPERTS, D_MODEL), lambda i: (0, 0)),
        pl.BlockSpec((N_EXPERTS, 1), lambda i: (0, 0)),
    ]
    out_specs = [
        pl.BlockSpec((MIX_TM, D_MODEL), lambda i: (i, 0)),
        pl.BlockSpec((MIX_TM, D_MODEL // 2), lambda i: (i, 0)),
        pl.BlockSpec((TOP_K, MIX_TM), lambda i: (0, i)),
        pl.BlockSpec((TOP_K, MIX_TM), lambda i: (0, i)),
        pl.BlockSpec((TOP_K, MIX_TM), lambda i: (0, i)),
        pl.BlockSpec((1, N_EXPERTS, LANES), lambda i: (i, 0, 0)),
    ]
    out_shape = [
        jax.ShapeDtypeStruct((N_TOK, D_MODEL), F32),
        jax.ShapeDtypeStruct((N_TOK, D_MODEL // 2), jnp.uint32),
        jax.ShapeDtypeStruct((TOP_K, N_TOK), jnp.int32),
        jax.ShapeDtypeStruct((TOP_K, N_TOK), F32),
        jax.ShapeDtypeStruct((TOP_K, N_TOK), jnp.int32),
        jax.ShapeDtypeStruct((n_tiles, N_EXPERTS, LANES), jnp.int32),
    ]
    return in_specs, out_specs, out_shape


def _mix_out_kernel(attn_ref, gb_ref, gc_ref, xin_ref, gcp_ref, xinp_ref, gcn_ref, xinn_ref,
                    cw_ref, wo_ref, x_ref, m_ref, lng_ref, lnb_ref, rwt_ref, rb_ref,
                    x1_ref, u2_ref, re_ref, rw_ref, rk_ref, cnt_ref):
    i = pl.program_id(0)
    tiles_per_seq = SEQ // MIX_TM
    z = gc_ref[...].astype(F32) * xin_ref[...].astype(F32)
    halo = 2 * SUBLANES
    zp_row = (gcp_ref[...].astype(F32) * xinp_ref[...].astype(F32))[halo - 1:halo, :]
    zn_row = (gcn_ref[...].astype(F32) * xinn_ref[...].astype(F32))[0:1, :]
    zp_row = jnp.where(i % tiles_per_seq == 0, 0.0, zp_row)
    zn_row = jnp.where(i % tiles_per_seq == tiles_per_seq - 1, 0.0, zn_row)
    row = lax.broadcasted_iota(jnp.int32, (MIX_TM, 1), 0)
    z_prev = jnp.where(row == 0, zp_row, pltpu.roll(z, 1, axis=0))
    z_next = jnp.where(row == MIX_TM - 1, zn_row, pltpu.roll(z, MIX_TM - 1, axis=0))
    conv = cw_ref[0:1, :] * z_prev + cw_ref[1:2, :] * z + cw_ref[2:3, :] * z_next
    gated = (gb_ref[...].astype(F32) * conv).astype(BF16)
    y = (jnp.dot(attn_ref[...], wo_ref[:ATTN_WIDTH, :], preferred_element_type=F32)
         + jnp.dot(gated, wo_ref[ATTN_WIDTH:, :], preferred_element_type=F32))
    _epilogue(y, x_ref, m_ref, lng_ref, lnb_ref, rwt_ref, rb_ref,
              x1_ref, u2_ref, re_ref, rw_ref, rk_ref, cnt_ref)


def _mix_out(attn, proj, conv_w, w_out, x2d, mods3, ln_g, ln_b, rwt, rb):
    tiles_per_seq = SEQ // MIX_TM
    halo = 2 * SUBLANES
    hb = MIX_TM // halo
    n_halo_blocks = N_TOK // halo
    cblk = lambda off: pl.BlockSpec((MIX_TM, CONV_CH), lambda i: (i, off))
    prev = lambda off: pl.BlockSpec((halo, CONV_CH), lambda i: (jnp.maximum(i * hb - 1, 0), off))
    nxt = lambda off: pl.BlockSpec(
        (halo, CONV_CH), lambda i: (jnp.minimum((i + 1) * hb, n_halo_blocks - 1), off))
    ep_in, out_specs, out_shape = _epilogue_specs(lambda i: i // tiles_per_seq)
    return pl.pallas_call(
        _mix_out_kernel,
        grid=(N_TOK // MIX_TM,),
        in_specs=[pl.BlockSpec((MIX_TM, ATTN_WIDTH), lambda i: (i, 0)),
                  cblk(3), cblk(4), cblk(5), prev(4), prev(5), nxt(4), nxt(5),
                  pl.BlockSpec((3, CONV_CH), lambda i: (0, 0)),
                  pl.BlockSpec((D_MODEL, D_MODEL), lambda i: (0, 0))] + ep_in,
        out_specs=out_specs,
        out_shape=out_shape,
        compiler_params=_cparams("arbitrary"),
        name="mix_out",
    )(attn, proj, proj, proj, proj, proj, proj, proj, conv_w, w_out, x2d, mods3, ln_g, ln_b, rwt, rb)


def _dft_seq_kernel(cs_ref, u_ref, o_ref):
    o_ref[...] = jnp.dot(cs_ref[...], u_ref[...], preferred_element_type=F32).astype(o_ref.dtype)


def _dft_seq(cs, u):
    m = 2 * SEQ
    return pl.pallas_call(
        _dft_seq_kernel,
        grid=(BATCH, D_MODEL // DFT_TN, m // DFT_TM),
        in_specs=[pl.BlockSpec((DFT_TM, SEQ), lambda b, n, i: (i, 0)),
                  pl.BlockSpec((SEQ, DFT_TN), lambda b, n, i: (b, n))],
        out_specs=pl.BlockSpec((DFT_TM, DFT_TN), lambda b, n, i: (b * (m // DFT_TM) + i, n)),
        out_shape=jax.ShapeDtypeStruct((BATCH * m, D_MODEL), BF16),
        compiler_params=_cparams("arbitrary", "arbitrary", "arbitrary"),
        name="dft_seq",
    )(cs, u)


def _four_out_kernel(p_ref, q_ref, cc_ref, sc_ref, wf_ref, x_ref, m_ref, lng_ref, lnb_ref,
                     rwt_ref, rb_ref, x1_ref, u2_ref, re_ref, rw_ref, rk_ref, cnt_ref):
    y = jnp.zeros((MIX_TM, D_MODEL), F32)
    for g in range(N_FOURIER_GROUPS):
        cols = slice(g * FOURIER_CH, (g + 1) * FOURIER_CH)
        fg = (jnp.dot(p_ref[:, cols], cc_ref[...], preferred_element_type=F32)
              - jnp.dot(q_ref[:, cols], sc_ref[...], preferred_element_type=F32))
        y = y + jnp.dot(fg.astype(BF16), wf_ref[cols, :], preferred_element_type=F32)
    _epilogue(y, x_ref, m_ref, lng_ref, lnb_ref, rwt_ref, rb_ref,
              x1_ref, u2_ref, re_ref, rw_ref, rk_ref, cnt_ref)


def _four_out(pq, cc, sc, wf, x2d, mods3, ln_g, ln_b, rwt, rb):
    tiles_per_seq = SEQ // MIX_TM
    ep_in, out_specs, out_shape = _epilogue_specs(lambda i: i // tiles_per_seq)
    prow = lambda i: (i // tiles_per_seq) * 2 * tiles_per_seq + i % tiles_per_seq
    return pl.pallas_call(
        _four_out_kernel,
        grid=(N_TOK // MIX_TM,),
        in_specs=[pl.BlockSpec((MIX_TM, D_MODEL), lambda i: (prow(i), 0)),
                  pl.BlockSpec((MIX_TM, D_MODEL), lambda i: (prow(i) + tiles_per_seq, 0)),
                  pl.BlockSpec((FOURIER_CH, FOURIER_CH), lambda i: (0, 0)),
                  pl.BlockSpec((FOURIER_CH, FOURIER_CH), lambda i: (0, 0)),
                  pl.BlockSpec((D_MODEL, D_MODEL), lambda i: (0, 0))] + ep_in,
        out_specs=out_specs,
        out_shape=out_shape,
        compiler_params=_cparams("arbitrary"),
        name="four_out",
    )(pq, pq, cc, sc, wf, x2d, mods3, ln_g, ln_b, rwt, rb)


def _dispatch_kernel(dest, fill_start, fill_n, u_ref, out_hbm, sem):
    i = pl.program_id(0)
    base = i * DISPATCH_TM

    def row_dma(r, d):
        return pltpu.make_async_copy(u_ref.at[pl.ds(r, 1)], out_hbm.at[pl.ds(d, 1)], sem)

    def issue(r, c):
        for k in range(TOP_K):
            row_dma(r, dest[k * N_TOK + base + r]).start()
        return c

    def drain(r, c):
        for k in range(TOP_K):
            row_dma(0, 0).wait()
        return c

    lax.fori_loop(0, DISPATCH_TM, issue, 0, unroll=8)
    lax.fori_loop(0, DISPATCH_TM, drain, 0, unroll=8)

    @pl.when(i == pl.num_programs(0) - 1)
    def _():
        def per_segment(e, c):
            lax.fori_loop(0, fill_n[e], lambda r, c2: (row_dma(0, fill_start[e] + r).start(), c2)[1], 0)
            lax.fori_loop(0, fill_n[e], lambda r, c2: (row_dma(0, 0).wait(), c2)[1], 0)
            return c

        lax.fori_loop(0, N_FILL_SEGMENTS, per_segment, 0)


def _dispatch(u2p, dest, fill_start, fill_n):
    return pl.pallas_call(
        _dispatch_kernel,
        grid_spec=pltpu.PrefetchScalarGridSpec(
            num_scalar_prefetch=3,
            grid=(N_TOK // DISPATCH_TM,),
            in_specs=[pl.BlockSpec((DISPATCH_TM, D_MODEL // 2), lambda i, *_: (i, 0))],
            out_specs=pl.BlockSpec(memory_space=pl.ANY),
            scratch_shapes=[pltpu.SemaphoreType.DMA(())],
        ),
        out_shape=jax.ShapeDtypeStruct((N_PAD, D_MODEL // 2), jnp.uint32),
        compiler_params=_cparams("arbitrary"),
        name="dispatch",
    )(dest, fill_start, fill_n, u2p)


def _expert_changed(be_ref):
    i = pl.program_id(0)
    return (i == 0) | (be_ref[i] != be_ref[jnp.maximum(i - 1, 0)])


def _experts_up_kernel(be_ref, nv_ref, x_ref, wg_ref, wu_ref, h_ref, wg_b, wu_b):
    @pl.when(_expert_changed(be_ref))
    def _():
        wg_b[...] = wg_ref[...].astype(BF16)
        wu_b[...] = wu_ref[...].astype(BF16)

    @pl.when(pl.program_id(0) >= nv_ref[0])
    def _():
        h_ref[...] = jnp.zeros_like(h_ref)

    @pl.when(pl.program_id(0) < nv_ref[0])
    def _():
        lo, hi = _unpack_bf16_pair(x_ref[...])
        half = D_MODEL // 2
        gate = (jnp.dot(lo, wg_b[:half, :], preferred_element_type=F32)
                + jnp.dot(hi, wg_b[half:, :], preferred_element_type=F32))
        up = (jnp.dot(lo, wu_b[:half, :], preferred_element_type=F32)
              + jnp.dot(hi, wu_b[half:, :], preferred_element_type=F32))
        h_ref[...] = (gate * _sigmoid(gate) * up).astype(BF16)


def _experts_down_kernel(be_ref, nv_ref, h_ref, wd_ref, o_ref, wd_b):
    @pl.when(_expert_changed(be_ref))
    def _():
        wd_b[...] = wd_ref[...].astype(BF16)

    @pl.when(pl.program_id(0) >= nv_ref[0])
    def _():
        o_ref[...] = jnp.zeros_like(o_ref)

    @pl.when(pl.program_id(0) < nv_ref[0])
    def _():
        o_ref[...] = jnp.dot(h_ref[...], wd_b[...], preferred_element_type=F32)


def _experts(layer, block_expert, n_valid, xs, w_gate, w_up, w_down):
    wspec = lambda r, c: pl.BlockSpec((None, None, r, c), lambda i, be, nv: (layer, be[i], 0, 0))
    rows = lambda c: pl.BlockSpec((EXPERT_TM, c), lambda i, be, nv: (i, 0))
    h = pl.pallas_call(
        _experts_up_kernel,
        grid_spec=pltpu.PrefetchScalarGridSpec(
            num_scalar_prefetch=2,
            grid=(N_EXPERT_BLOCKS,),
            in_specs=[rows(D_MODEL // 2), wspec(D_MODEL, D_EXPERT), wspec(D_MODEL, D_EXPERT)],
            out_specs=rows(D_EXPERT),
            scratch_shapes=[pltpu.VMEM((D_MODEL, D_EXPERT), BF16), pltpu.VMEM((D_MODEL, D_EXPERT), BF16)],
        ),
        out_shape=jax.ShapeDtypeStruct((N_PAD, D_EXPERT), BF16),
        compiler_params=_cparams("arbitrary"),
        name="experts_up",
    )(block_expert, n_valid, xs, w_gate, w_up)
    return pl.pallas_call(
        _experts_down_kernel,
        grid_spec=pltpu.PrefetchScalarGridSpec(
            num_scalar_prefetch=2,
            grid=(N_EXPERT_BLOCKS,),
            in_specs=[rows(D_EXPERT), wspec(D_EXPERT, D_MODEL)],
            out_specs=rows(D_MODEL),
            scratch_shapes=[pltpu.VMEM((D_EXPERT, D_MODEL), BF16)],
        ),
        out_shape=jax.ShapeDtypeStruct((N_PAD, D_MODEL), F32),
        compiler_params=_cparams("arbitrary"),
        name="experts_down",
    )(block_expert, n_valid, h, w_down)


def _final_kernel(dest, ys_hbm, rw_ref, x1_ref, m_ref, lng_ref, lnb_ref, *rest, emit_next):
    if emit_next:
        mn_ref, x2_ref, un_ref, buf, sem = rest
    else:
        x2_ref, buf, sem = rest
    i = pl.program_id(0)
    n = pl.num_programs(0)

    def row_dma(tile, slot, r, k):
        src = ys_hbm.at[pl.ds(dest[k * N_TOK + tile * MIX_TM + r], 1)]
        return pltpu.make_async_copy(src, buf.at[slot, k, pl.ds(r, 1)], sem.at[slot])

    def gather(tile, slot):
        def issue(r, c):
            for k in range(TOP_K):
                row_dma(tile, slot, r, k).start()
            return c
        lax.fori_loop(0, MIX_TM, issue, 0, unroll=8)

    @pl.when(i == 0)
    def _():
        gather(0, 0)

    @pl.when(i + 1 < n)
    def _():
        gather(i + 1, (i + 1) % 2)

    slot = i % 2

    def drain(r, c):
        for k in range(TOP_K):
            pltpu.make_async_copy(ys_hbm.at[pl.ds(0, 1)], buf.at[slot, k, pl.ds(0, 1)], sem.at[slot]).wait()
        return c
    lax.fori_loop(0, MIX_TM, drain, 0, unroll=8)

    m = m_ref[0]
    g2 = m[:, 5 * D_MODEL:6 * D_MODEL]
    f = rw_ref[:, 0:1] * buf[slot, 0] + rw_ref[:, 1:2] * buf[slot, 1]
    x2 = _ln(ALPHA * x1_ref[...] + g2 * f) * lng_ref[1:2, :] + lnb_ref[1:2, :]
    x2_ref[...] = x2
    if emit_next:
        mn = mn_ref[0]
        un_ref[...] = (_ln(x2) * (1.0 + mn[:, D_MODEL:2 * D_MODEL]) + mn[:, :D_MODEL]).astype(BF16)


def _final(dest, ys, rw_t, x1, mods3, ln_g, ln_b, mods3_next=None):
    tiles_per_seq = SEQ // MIX_TM
    emit_next = mods3_next is not None
    mspec = pl.BlockSpec((1, 1, 6 * D_MODEL), lambda i, d: (i // tiles_per_seq, 0, 0))
    row = pl.BlockSpec((MIX_TM, D_MODEL), lambda i, d: (i, 0))
    in_specs = [pl.BlockSpec(memory_space=pl.ANY),
                pl.BlockSpec((MIX_TM, TOP_K), lambda i, d: (i, 0)),
                row, mspec,
                pl.BlockSpec((2, D_MODEL), lambda i, d: (0, 0)),
                pl.BlockSpec((2, D_MODEL), lambda i, d: (0, 0))]
    args = [ys, rw_t, x1, mods3, ln_g, ln_b]
    out_specs = [row]
    out_shape = [jax.ShapeDtypeStruct((N_TOK, D_MODEL), F32)]
    if emit_next:
        in_specs.append(mspec)
        args.append(mods3_next)
        out_specs.append(row)
        out_shape.append(jax.ShapeDtypeStruct((N_TOK, D_MODEL), BF16))
    return pl.pallas_call(
        functools.partial(_final_kernel, emit_next=emit_next),
        grid_spec=pltpu.PrefetchScalarGridSpec(
            num_scalar_prefetch=1,
            grid=(N_TOK // MIX_TM,),
            in_specs=in_specs,
            out_specs=out_specs,
            scratch_shapes=[pltpu.VMEM((2, TOP_K, MIX_TM, D_MODEL), F32), pltpu.SemaphoreType.DMA((2,))],
        ),
        out_shape=out_shape,
        compiler_params=_cparams("arbitrary"),
        name="final",
    )(dest, *args)


def _moe(layer, u2p, route_e, route_w, rank, cnt, w_gate, w_up, w_down):
    n_tiles = N_TOK // MIX_TM
    cnt = cnt[:, :, 0]
    counts = jnp.sum(cnt, axis=0)
    padded = (counts + EXPERT_TM - 1) // EXPERT_TM * EXPERT_TM
    pad_end = jnp.cumsum(padded)
    pad_start = pad_end - padded
    base = pad_start[None, :] + jnp.cumsum(cnt, axis=0) - cnt
    e_r = route_e.reshape(TOP_K, n_tiles, MIX_TM)
    onehot = e_r[..., None] == jnp.arange(N_EXPERTS, dtype=jnp.int32)
    dest = jnp.sum(jnp.where(onehot, base[None, :, None, :], 0), axis=-1).reshape(TOP_K, N_TOK) + rank
    dest = dest.reshape(-1).astype(jnp.int32)
    n_valid = (pad_end[-1] // EXPERT_TM).astype(jnp.int32).reshape(1)
    block_row0 = jnp.arange(N_EXPERT_BLOCKS, dtype=jnp.int32) * EXPERT_TM
    block_expert = jnp.minimum(
        jnp.sum(pad_end[None, :] <= block_row0[:, None], axis=1), N_EXPERTS - 1).astype(jnp.int32)

    fill_start = jnp.concatenate([pad_start + counts, pad_end[-1:]]).astype(jnp.int32)
    fill_n = jnp.concatenate([padded - counts, N_PAD - pad_end[-1:]]).astype(jnp.int32)
    xs = _dispatch(u2p, dest, fill_start, fill_n)
    ys = _experts(layer, block_expert, n_valid, xs, w_gate, w_up, w_down)
    return dest, ys, route_w.T


def _dft_tables():
    k = np.arange(SEQ, dtype=np.float64)[:, None]
    n1 = np.arange(GRID_W, dtype=np.float64)[None, :]
    coarse = 2 * np.pi * ((k * n1) % GRID_W) / GRID_W
    fine = 2 * np.pi * k * n1 / SEQ
    ac, as_ = jnp.asarray(np.cos(coarse), F32), jnp.asarray(np.sin(coarse), F32)
    bc, bs = jnp.asarray(np.cos(fine), F32), jnp.asarray(np.sin(fine), F32)
    s = SEQ ** -0.5
    cos = (ac[:, :, None] * bc[:, None, :] - as_[:, :, None] * bs[:, None, :]) * s
    sin = (as_[:, :, None] * bc[:, None, :] + ac[:, :, None] * bs[:, None, :]) * s
    cs = jnp.concatenate([cos.reshape(SEQ, SEQ), sin.reshape(SEQ, SEQ)], axis=0).astype(BF16)
    kc = np.arange(FOURIER_CH, dtype=np.float64)
    ang = 2 * np.pi * ((kc[:, None] * kc[None, :]) % FOURIER_CH) / FOURIER_CH
    cc = jnp.asarray(np.cos(ang) * FOURIER_CH ** -0.5, F32).astype(BF16)
    sc = jnp.asarray(np.sin(ang) * FOURIER_CH ** -0.5, F32).astype(BF16)
    return cs, cc, sc


def kernel(x, c, ctx, c_ctx, ada_w, ada_b, w_mix_in, rpb, conv_w, w_mix_out, w_fourier_out,
           router_w, router_b, w_gate, w_up, w_down, ln_g, ln_b):
    x2d = x.reshape(N_TOK, D_MODEL)
    c8 = jnp.concatenate(
        [c, c_ctx[None, :], jnp.zeros((SUBLANES - BATCH - 1, D_MODEL), F32)], axis=0)
    mods = _mods(c8, ada_w, ada_b)
    mods3 = [mods[l].reshape(SUBLANES, 1, 6 * D_MODEL) for l in range(DEPTH)]
    perm = lambda a: a.reshape(N_GROUPS, EXPERTS_PER_GROUP, -1).transpose(1, 0, 2).reshape(N_EXPERTS, -1)
    rwt = perm(router_w.T)
    rb = perm(router_b.reshape(N_EXPERTS, 1))
    tiles_per_seq = SEQ // PROJ_TM

    w_in = w_mix_in[0].astype(BF16)
    proj = _ln_proj(x2d, mods3[0], w_in, 0, MIX_IN_WIDTH, lambda i: i // tiles_per_seq)
    kvc = _ln_proj(ctx.reshape(BATCH * CTX_LEN, D_MODEL), mods3[0], w_in,
                   ATTN_WIDTH // PROJ_TN, 2 * ATTN_WIDTH, lambda i: BATCH)
    attn = _natten(proj, kvc, _bias_table(rpb[0]))
    x1, u2p, route_e, route_w, rank, cnt = _mix_out(
        attn, proj, conv_w[0], w_mix_out[0].astype(BF16), x2d, mods3[0], ln_g[0], ln_b[0], rwt, rb)
    dest, ys, rw_t = _moe(0, u2p, route_e, route_w, rank, cnt, w_gate, w_up, w_down)
    x2, u_next = _final(dest, ys, rw_t, x1, mods3[0], ln_g[0], ln_b[0], mods3_next=mods3[1])

    cs, cc, sc = _dft_tables()
    pq = _dft_seq(cs, u_next)
    x1, u2p, route_e, route_w, rank, cnt = _four_out(
        pq, cc, sc, w_fourier_out[0].astype(BF16), x2, mods3[1], ln_g[1], ln_b[1], rwt, rb)
    dest, ys, rw_t = _moe(1, u2p, route_e, route_w, rank, cnt, w_gate, w_up, w_down)
    (x3,) = _final(dest, ys, rw_t, x1, mods3[1], ln_g[1], ln_b[1])
    return x3.reshape(BATCH, SEQ, D_MODEL)
```

```python
import functools

import numpy as np
import jax
import jax.numpy as jnp
from jax import lax
from jax.experimental import pallas as pl
from jax.experimental.pallas import tpu as pltpu

D_MODEL = 2048
BATCH = 4
SEQ = 4096
DEPTH = 2
GRID_W = 64
GRID_ROWS = SEQ // GRID_W
CTX_LEN = 256
HEAD_DIM = 128
ATTN_WIDTH = D_MODEL // 2
N_HEADS = ATTN_WIDTH // HEAD_DIM
WIN_H = 8
WIN_W = 16
CONV_CH = D_MODEL // 2
MIX_IN_WIDTH = 3 * ATTN_WIDTH + 3 * CONV_CH
N_FOURIER_GROUPS = 4
FOURIER_CH = D_MODEL // N_FOURIER_GROUPS
N_EXPERTS = 32
N_GROUPS = 8
EXPERTS_PER_GROUP = N_EXPERTS // N_GROUPS
TOP_K = 2
D_EXPERT = D_MODEL // 2
ALPHA = (2 * DEPTH) ** 0.25
LN_EPS = 1e-6
NEG_INF = -1e30
N_TOK = BATCH * SEQ

V7X_VMEM_LIMIT_BYTES = 56 * 1024 * 1024
SUBLANES = 8
LANES = 128

PROJ_TM = 512
PROJ_TN = 1024
ATT_QROWS = 8
ATT_KROWS = 16
ATT_Q = ATT_QROWS * GRID_W
ATT_K = ATT_KROWS * GRID_W
N_QBLOCKS = GRID_ROWS // ATT_QROWS
MIX_TM = 256
DFT_TM = 1024
DFT_TN = 1024
EXPERT_TM = 256
N_ROWS = N_TOK * TOP_K
N_EXPERT_BLOCKS = N_ROWS // EXPERT_TM + N_EXPERTS
N_PAD = N_EXPERT_BLOCKS * EXPERT_TM
DISPATCH_TM = 1024
N_FILL_SEGMENTS = N_EXPERTS + 1
MODS_TN = 1024

F32 = jnp.float32
BF16 = jnp.bfloat16


def _cparams(*sem):
    return pltpu.CompilerParams(dimension_semantics=sem, vmem_limit_bytes=V7X_VMEM_LIMIT_BYTES)


def _ln(x):
    mu = jnp.mean(x, axis=-1, keepdims=True)
    xc = x - mu
    var = jnp.mean(xc * xc, axis=-1, keepdims=True)
    return xc * lax.rsqrt(var + LN_EPS)


def _sigmoid(x):
    return 1.0 / (1.0 + jnp.exp(-x))


def _mods_kernel(c_ref, w_ref, b_ref, o_ref):
    c = c_ref[...]
    s = (c * _sigmoid(c)).astype(BF16)
    o_ref[0] = jnp.dot(s, w_ref[0].astype(BF16), preferred_element_type=F32) + b_ref[0]


def _mods(c8, ada_w, ada_b):
    n = 6 * D_MODEL
    return pl.pallas_call(
        _mods_kernel,
        grid=(DEPTH, n // MODS_TN),
        in_specs=[
            pl.BlockSpec((SUBLANES, D_MODEL), lambda l, j: (0, 0)),
            pl.BlockSpec((1, D_MODEL, MODS_TN), lambda l, j: (l, 0, j)),
            pl.BlockSpec((1, 1, MODS_TN), lambda l, j: (l, 0, j)),
        ],
        out_specs=pl.BlockSpec((1, SUBLANES, MODS_TN), lambda l, j: (l, 0, j)),
        out_shape=jax.ShapeDtypeStruct((DEPTH, SUBLANES, n), F32),
        compiler_params=_cparams("arbitrary", "arbitrary"),
        name="mods",
    )(c8, ada_w, ada_b.reshape(DEPTH, 1, n))


def _ln_proj_kernel(x_ref, m_ref, w_ref, o_ref, u_scr):
    @pl.when(pl.program_id(1) == 0)
    def _():
        m = m_ref[0]
        y = _ln(x_ref[...])
        u_scr[...] = (y * (1.0 + m[:, D_MODEL:2 * D_MODEL]) + m[:, :D_MODEL]).astype(BF16)

    o_ref[...] = jnp.dot(u_scr[...], w_ref[...], preferred_element_type=F32).astype(o_ref.dtype)


def _ln_proj(x2d, mods3, w, col_block0, n_cols, mod_row_of_tile):
    n_rows = x2d.shape[0]
    return pl.pallas_call(
        _ln_proj_kernel,
        grid=(n_rows // PROJ_TM, n_cols // PROJ_TN),
        in_specs=[
            pl.BlockSpec((PROJ_TM, D_MODEL), lambda i, j: (i, 0)),
            pl.BlockSpec((1, 1, 6 * D_MODEL), lambda i, j: (mod_row_of_tile(i), 0, 0)),
            pl.BlockSpec((D_MODEL, PROJ_TN), lambda i, j: (0, col_block0 + j)),
        ],
        out_specs=pl.BlockSpec((PROJ_TM, PROJ_TN), lambda i, j: (i, j)),
        out_shape=jax.ShapeDtypeStruct((n_rows, n_cols), BF16),
        scratch_shapes=[pltpu.VMEM((PROJ_TM, D_MODEL), BF16)],
        compiler_params=_cparams("arbitrary", "arbitrary"),
        name="ln_proj",
    )(x2d, mods3, w)


def _bias_table_kernel(rpb_ref, o_ref):
    h = pl.program_id(0)
    qc = lax.broadcasted_iota(jnp.int32, (GRID_W, GRID_W), 0)
    kc = lax.broadcasted_iota(jnp.int32, (GRID_W, GRID_W), 1)
    col_start = jnp.clip(qc - WIN_W // 2, 0, GRID_W - WIN_W)
    col_ok = (kc >= col_start) & (kc < col_start + WIN_W)
    col_idx = jnp.clip(kc - qc, -(WIN_W - 1), WIN_W - 1) + WIN_W - 1
    n_rb = 2 * WIN_H - 1
    n_cb = 2 * WIN_W - 1
    tiles = []
    for ri in range(n_rb):
        acc = jnp.zeros((GRID_W, GRID_W), F32)
        for i in range(n_cb):
            acc = jnp.where(col_idx == i, rpb_ref[(h * n_rb + ri) * n_cb + i], acc)
        tiles.append(jnp.where(col_ok, acc, NEG_INF))
    masked = jnp.full((GRID_W, GRID_W), NEG_INF, F32)
    for typ, blk in enumerate((0, 1, N_QBLOCKS - 1)):
        ks = min(max(blk * ATT_QROWS - WIN_H // 2, 0), GRID_ROWS - ATT_KROWS)
        for a in range(ATT_QROWS):
            r = blk * ATT_QROWS + a
            rs = min(max(r - WIN_H // 2, 0), GRID_ROWS - WIN_H)
            strip = [tiles[ks + c - r + WIN_H - 1] if rs <= ks + c < rs + WIN_H else masked
                     for c in range(ATT_KROWS)]
            o_ref[0, typ, a * GRID_W:(a + 1) * GRID_W, :] = jnp.concatenate(strip, axis=1)


def _bias_table(rpb_l):
    flat = rpb_l.reshape(-1).astype(F32)
    return pl.pallas_call(
        _bias_table_kernel,
        grid=(N_HEADS,),
        in_specs=[pl.BlockSpec(memory_space=pltpu.SMEM)],
        out_specs=pl.BlockSpec((1, 3, ATT_Q, ATT_K), lambda h: (h, 0, 0, 0)),
        out_shape=jax.ShapeDtypeStruct((N_HEADS, 3, ATT_Q, ATT_K), F32),
        compiler_params=_cparams("arbitrary"),
        name="bias_table",
    )(flat)


def _natten_kernel(q_ref, k_ref, v_ref, kc_ref, vc_ref, bias_ref, o_ref):
    scale = HEAD_DIM ** -0.5
    kc = kc_ref[...]
    vc = vc_ref[...]
    nt = (((1,), (1,)), ((), ()))

    def body(j, carry):
        typ = jnp.where(j == 0, 0, jnp.where(j == N_QBLOCKS - 1, 2, 1))
        q0 = pl.multiple_of(j * ATT_Q, ATT_Q)
        k0 = pl.multiple_of(
            jnp.clip(j * ATT_QROWS - WIN_H // 2, 0, GRID_ROWS - ATT_KROWS) * GRID_W, 4 * GRID_W)
        q = q_ref[pl.ds(q0, ATT_Q), :]
        kw = k_ref[pl.ds(k0, ATT_K), :]
        vw = v_ref[pl.ds(k0, ATT_K), :]
        s_loc = lax.dot_general(q, kw, nt, preferred_element_type=F32) * scale + bias_ref[typ]
        s_ctx = lax.dot_general(q, kc, nt, preferred_element_type=F32) * scale
        m = jnp.maximum(jnp.max(s_loc, axis=-1, keepdims=True), jnp.max(s_ctx, axis=-1, keepdims=True))
        p_loc = jnp.exp(s_loc - m)
        p_ctx = jnp.exp(s_ctx - m)
        denom = jnp.sum(p_loc, axis=-1, keepdims=True) + jnp.sum(p_ctx, axis=-1, keepdims=True)
        o = (jnp.dot(p_loc.astype(BF16), vw, preferred_element_type=F32)
             + jnp.dot(p_ctx.astype(BF16), vc, preferred_element_type=F32))
        o_ref[pl.ds(q0, ATT_Q), :] = (o / denom).astype(o_ref.dtype)
        return carry

    lax.fori_loop(0, N_QBLOCKS, body, 0)


def _natten(proj, kvc, bias):
    blk = lambda off: pl.BlockSpec((SEQ, HEAD_DIM), lambda b, h: (b, off + h))
    cblk = lambda off: pl.BlockSpec((CTX_LEN, HEAD_DIM), lambda b, h: (b, off + h))
    return pl.pallas_call(
        _natten_kernel,
        grid=(BATCH, N_HEADS),
        in_specs=[blk(0), blk(N_HEADS), blk(2 * N_HEADS), cblk(0), cblk(N_HEADS),
                  pl.BlockSpec((None, 3, ATT_Q, ATT_K), lambda b, h: (h, 0, 0, 0))],
        out_specs=pl.BlockSpec((SEQ, HEAD_DIM), lambda b, h: (b, h)),
        out_shape=jax.ShapeDtypeStruct((N_TOK, ATTN_WIDTH), BF16),
        compiler_params=_cparams("arbitrary", "arbitrary"),
        name="natten",
    )(proj, proj, proj, kvc, kvc, bias)


def _pack_bf16_pair(lo, hi):
    return pltpu.pack_elementwise([lo, hi], packed_dtype=BF16)


def _unpack_bf16_pair(w):
    unpack = lambda i: pltpu.unpack_elementwise(w, index=i, packed_dtype=BF16, unpacked_dtype=F32)
    return unpack(0).astype(BF16), unpack(1).astype(BF16)


def _epilogue(y, x_ref, m_ref, lng_ref, lnb_ref, rwt_ref, rb_ref,
              x1_ref, u2_ref, re_ref, rw_ref, rk_ref, cnt_ref):
    m = m_ref[0]
    g1 = m[:, 2 * D_MODEL:3 * D_MODEL]
    sh2 = m[:, 3 * D_MODEL:4 * D_MODEL]
    sc2 = m[:, 4 * D_MODEL:5 * D_MODEL]
    x1 = _ln(ALPHA * x_ref[...] + g1 * y) * lng_ref[0:1, :] + lnb_ref[0:1, :]
    x1_ref[...] = x1
    u2 = _ln(x1) * (1.0 + sc2) + sh2
    u2_ref[...] = _pack_bf16_pair(u2[:, :D_MODEL // 2], u2[:, D_MODEL // 2:])

    tm = u2.shape[0]
    logits = lax.dot_general(rwt_ref[...], u2, (((1,), (1,)), ((), ())),
                             precision=lax.Precision.HIGHEST, preferred_element_type=F32)
    aff = _sigmoid(logits)
    biased = aff + rb_ref[...]
    a = [biased[l * N_GROUPS:(l + 1) * N_GROUPS, :] for l in range(EXPERTS_PER_GROUP)]
    f = [aff[l * N_GROUPS:(l + 1) * N_GROUPS, :] for l in range(EXPERTS_PER_GROUP)]
    hi01, lo01 = jnp.maximum(a[0], a[1]), jnp.minimum(a[0], a[1])
    hi23, lo23 = jnp.maximum(a[2], a[3]), jnp.minimum(a[2], a[3])
    top1 = jnp.maximum(hi01, hi23)
    top2 = jnp.maximum(jnp.minimum(hi01, hi23), jnp.maximum(lo01, lo23))
    gscore = top1 + top2
    gi = lax.broadcasted_iota(jnp.int32, (N_GROUPS, tm), 0)
    gmax = jnp.max(gscore, axis=0, keepdims=True)
    group = jnp.min(jnp.where(gscore == gmax, gi, N_GROUPS), axis=0, keepdims=True)
    sel = gi == group
    b = [jnp.sum(jnp.where(sel, a[l], 0.0), axis=0, keepdims=True) for l in range(EXPERTS_PER_GROUP)]
    c = [jnp.sum(jnp.where(sel, f[l], 0.0), axis=0, keepdims=True) for l in range(EXPERTS_PER_GROUP)]

    def first_argmax(vals):
        mx = functools.reduce(jnp.maximum, vals)
        idx = jnp.full(mx.shape, len(vals) - 1, jnp.int32)
        for l in range(len(vals) - 2, -1, -1):
            idx = jnp.where(vals[l] == mx, l, idx)
        return idx

    i1 = first_argmax(b)
    i2 = first_argmax([jnp.where(i1 == l, -jnp.inf, b[l]) for l in range(EXPERTS_PER_GROUP)])
    pick = lambda idx: functools.reduce(
        lambda acc, l: jnp.where(idx == l, c[l], acc), range(1, EXPERTS_PER_GROUP), c[0])
    w1, w2 = pick(i1), pick(i2)
    wsum = w1 + w2
    e1 = group * EXPERTS_PER_GROUP + i1
    e2 = group * EXPERTS_PER_GROUP + i2
    re_ref[0:1, :] = e1
    re_ref[1:2, :] = e2
    rw_ref[0:1, :] = w1 / wsum
    rw_ref[1:2, :] = w2 / wsum

    ei = lax.broadcasted_iota(jnp.int32, (N_EXPERTS, tm), 0)
    oh1 = ei == e1
    oh2 = ei == e2
    oh = jnp.where(oh1 | oh2, 1.0, 0.0)
    t_row = lax.broadcasted_iota(jnp.int32, (tm, tm), 0)
    t_col = lax.broadcasted_iota(jnp.int32, (tm, tm), 1)
    before = jnp.where(t_row < t_col, 1.0, 0.0).astype(BF16)
    prefix = jnp.dot(oh.astype(BF16), before, preferred_element_type=F32)
    rk_ref[0:1, :] = jnp.sum(jnp.where(oh1, prefix, 0.0), axis=0, keepdims=True).astype(jnp.int32)
    rk_ref[1:2, :] = jnp.sum(jnp.where(oh2, prefix, 0.0), axis=0, keepdims=True).astype(jnp.int32)
    cnt = jnp.sum(oh, axis=1, keepdims=True)
    cnt_ref[0] = jnp.broadcast_to(cnt, (N_EXPERTS, LANES)).astype(jnp.int32)


def _epilogue_specs(tile_batch):
    n_tiles = N_TOK // MIX_TM
    in_specs = [
        pl.BlockSpec((MIX_TM, D_MODEL), lambda i: (i, 0)),
        pl.BlockSpec((1, 1, 6 * D_MODEL), lambda i: (tile_batch(i), 0, 0)),
        pl.BlockSpec((2, D_MODEL), lambda i: (0, 0)),
        pl.BlockSpec((2, D_MODEL), lambda i: (0, 0)),
        pl.BlockSpec((N_EXPERTS, D_MODEL), lambda i: (0, 0)),
        pl.BlockSpec((N_EXPERTS, 1), lambda i: (0, 0)),
    ]
    out_specs = [
        pl.BlockSpec((MIX_TM, D_MODEL), lambda i: (i, 0)),
        pl.BlockSpec((MIX_TM, D_MODEL // 2), lambda i: (i, 0)),
        pl.BlockSpec((TOP_K, MIX_TM), lambda i: (0, i)),
        pl.BlockSpec((TOP_K, MIX_TM), lambda i: (0, i)),
        pl.BlockSpec((TOP_K, MIX_TM), lambda i: (0, i)),
        pl.BlockSpec((1, N_EXPERTS, LANES), lambda i: (i, 0, 0)),
    ]
    out_shape = [
        jax.ShapeDtypeStruct((N_TOK, D_MODEL), F32),
        jax.ShapeDtypeStruct((N_TOK, D_MODEL // 2), jnp.uint32),
        jax.ShapeDtypeStruct((TOP_K, N_TOK), jnp.int32),
        jax.ShapeDtypeStruct((TOP_K, N_TOK), F32),
        jax.ShapeDtypeStruct((TOP_K, N_TOK), jnp.int32),
        jax.ShapeDtypeStruct((n_tiles, N_EXPERTS, LANES), jnp.int32),
    ]
    return in_specs, out_specs, out_shape


def _mix_out_kernel(attn_ref, gb_ref, gc_ref, xin_ref, gcp_ref, xinp_ref, gcn_ref, xinn_ref,
                    cw_ref, wo_ref, x_ref, m_ref, lng_ref, lnb_ref, rwt_ref, rb_ref,
                    x1_ref, u2_ref, re_ref, rw_ref, rk_ref, cnt_ref):
    i = pl.program_id(0)
    tiles_per_seq = SEQ // MIX_TM
    z = gc_ref[...].astype(F32) * xin_ref[...].astype(F32)
    halo = 2 * SUBLANES
    zp_row = (gcp_ref[...].astype(F32) * xinp_ref[...].astype(F32))[halo - 1:halo, :]
    zn_row = (gcn_ref[...].astype(F32) * xinn_ref[...].astype(F32))[0:1, :]
    zp_row = jnp.where(i % tiles_per_seq == 0, 0.0, zp_row)
    zn_row = jnp.where(i % tiles_per_seq == tiles_per_seq - 1, 0.0, zn_row)
    row = lax.broadcasted_iota(jnp.int32, (MIX_TM, 1), 0)
    z_prev = jnp.where(row == 0, zp_row, pltpu.roll(z, 1, axis=0))
    z_next = jnp.where(row == MIX_TM - 1, zn_row, pltpu.roll(z, MIX_TM - 1, axis=0))
    conv = cw_ref[0:1, :] * z_prev + cw_ref[1:2, :] * z + cw_ref[2:3, :] * z_next
    gated = (gb_ref[...].astype(F32) * conv).astype(BF16)
    y = (jnp.dot(attn_ref[...], wo_ref[:ATTN_WIDTH, :], preferred_element_type=F32)
         + jnp.dot(gated, wo_ref[ATTN_WIDTH:, :], preferred_element_type=F32))
    _epilogue(y, x_ref, m_ref, lng_ref, lnb_ref, rwt_ref, rb_ref,
              x1_ref, u2_ref, re_ref, rw_ref, rk_ref, cnt_ref)


def _mix_out(attn, proj, conv_w, w_out, x2d, mods3, ln_g, ln_b, rwt, rb):
    tiles_per_seq = SEQ // MIX_TM
    halo = 2 * SUBLANES
    hb = MIX_TM // halo
    n_halo_blocks = N_TOK // halo
    cblk = lambda off: pl.BlockSpec((MIX_TM, CONV_CH), lambda i: (i, off))
    prev = lambda off: pl.BlockSpec((halo, CONV_CH), lambda i: (jnp.maximum(i * hb - 1, 0), off))
    nxt = lambda off: pl.BlockSpec(
        (halo, CONV_CH), lambda i: (jnp.minimum((i + 1) * hb, n_halo_blocks - 1), off))
    ep_in, out_specs, out_shape = _epilogue_specs(lambda i: i // tiles_per_seq)
    return pl.pallas_call(
        _mix_out_kernel,
        grid=(N_TOK // MIX_TM,),
        in_specs=[pl.BlockSpec((MIX_TM, ATTN_WIDTH), lambda i: (i, 0)),
                  cblk(3), cblk(4), cblk(5), prev(4), prev(5), nxt(4), nxt(5),
                  pl.BlockSpec((3, CONV_CH), lambda i: (0, 0)),
                  pl.BlockSpec((D_MODEL, D_MODEL), lambda i: (0, 0))] + ep_in,
        out_specs=out_specs,
        out_shape=out_shape,
        compiler_params=_cparams("arbitrary"),
        name="mix_out",
    )(attn, proj, proj, proj, proj, proj, proj, proj, conv_w, w_out, x2d, mods3, ln_g, ln_b, rwt, rb)


def _dft_seq_kernel(cs_ref, u_ref, o_ref):
    o_ref[...] = jnp.dot(cs_ref[...], u_ref[...], preferred_element_type=F32).astype(o_ref.dtype)


def _dft_seq(cs, u):
    m = 2 * SEQ
    return pl.pallas_call(
        _dft_seq_kernel,
        grid=(BATCH, D_MODEL // DFT_TN, m // DFT_TM),
        in_specs=[pl.BlockSpec((DFT_TM, SEQ), lambda b, n, i: (i, 0)),
                  pl.BlockSpec((SEQ, DFT_TN), lambda b, n, i: (b, n))],
        out_specs=pl.BlockSpec((DFT_TM, DFT_TN), lambda b, n, i: (b * (m // DFT_TM) + i, n)),
        out_shape=jax.ShapeDtypeStruct((BATCH * m, D_MODEL), BF16),
        compiler_params=_cparams("arbitrary", "arbitrary", "arbitrary"),
        name="dft_seq",
    )(cs, u)


def _four_out_kernel(p_ref, q_ref, cc_ref, sc_ref, wf_ref, x_ref, m_ref, lng_ref, lnb_ref,
                     rwt_ref, rb_ref, x1_ref, u2_ref, re_ref, rw_ref, rk_ref, cnt_ref):
    y = jnp.zeros((MIX_TM, D_MODEL), F32)
    for g in range(N_FOURIER_GROUPS):
        cols = slice(g * FOURIER_CH, (g + 1) * FOURIER_CH)
        fg = (jnp.dot(p_ref[:, cols], cc_ref[...], preferred_element_type=F32)
              - jnp.dot(q_ref[:, cols], sc_ref[...], preferred_element_type=F32))
        y = y + jnp.dot(fg.astype(BF16), wf_ref[cols, :], preferred_element_type=F32)
    _epilogue(y, x_ref, m_ref, lng_ref, lnb_ref, rwt_ref, rb_ref,
              x1_ref, u2_ref, re_ref, rw_ref, rk_ref, cnt_ref)


def _four_out(pq, cc, sc, wf, x2d, mods3, ln_g, ln_b, rwt, rb):
    tiles_per_seq = SEQ // MIX_TM
    ep_in, out_specs, out_shape = _epilogue_specs(lambda i: i // tiles_per_seq)
    prow = lambda i: (i // tiles_per_seq) * 2 * tiles_per_seq + i % tiles_per_seq
    return pl.pallas_call(
        _four_out_kernel,
        grid=(N_TOK // MIX_TM,),
        in_specs=[pl.BlockSpec((MIX_TM, D_MODEL), lambda i: (prow(i), 0)),
                  pl.BlockSpec((MIX_TM, D_MODEL), lambda i: (prow(i) + tiles_per_seq, 0)),
                  pl.BlockSpec((FOURIER_CH, FOURIER_CH), lambda i: (0, 0)),
                  pl.BlockSpec((FOURIER_CH, FOURIER_CH), lambda i: (0, 0)),
                  pl.BlockSpec((D_MODEL, D_MODEL), lambda i: (0, 0))] + ep_in,
        out_specs=out_specs,
        out_shape=out_shape,
        compiler_params=_cparams("arbitrary"),
        name="four_out",
    )(pq, pq, cc, sc, wf, x2d, mods3, ln_g, ln_b, rwt, rb)


def _dispatch_kernel(dest, fill_start, fill_n, u_ref, out_hbm, sem):
    i = pl.program_id(0)
    base = i * DISPATCH_TM

    def row_dma(r, d):
        return pltpu.make_async_copy(u_ref.at[pl.ds(r, 1)], out_hbm.at[pl.ds(d, 1)], sem)

    def issue(r, c):
        for k in range(TOP_K):
            row_dma(r, dest[k * N_TOK + base + r]).start(priority=k)
        return c

    def drain(r, c):
        for k in range(TOP_K):
            row_dma(0, 0).wait()
        return c

    lax.fori_loop(0, DISPATCH_TM, issue, 0, unroll=8)
    lax.fori_loop(0, DISPATCH_TM, drain, 0, unroll=8)

    @pl.when(i == pl.num_programs(0) - 1)
    def _():
        def per_segment(e, c):
            lax.fori_loop(0, fill_n[e], lambda r, c2: (row_dma(0, fill_start[e] + r).start(), c2)[1], 0)
            lax.fori_loop(0, fill_n[e], lambda r, c2: (row_dma(0, 0).wait(), c2)[1], 0)
            return c

        lax.fori_loop(0, N_FILL_SEGMENTS, per_segment, 0)


def _dispatch(u2p, dest, fill_start, fill_n):
    return pl.pallas_call(
        _dispatch_kernel,
        grid_spec=pltpu.PrefetchScalarGridSpec(
            num_scalar_prefetch=3,
            grid=(N_TOK // DISPATCH_TM,),
            in_specs=[pl.BlockSpec((DISPATCH_TM, D_MODEL // 2), lambda i, *_: (i, 0))],
            out_specs=pl.BlockSpec(memory_space=pl.ANY),
            scratch_shapes=[pltpu.SemaphoreType.DMA(())],
        ),
        out_shape=jax.ShapeDtypeStruct((N_PAD, D_MODEL // 2), jnp.uint32),
        compiler_params=_cparams("arbitrary"),
        name="dispatch",
    )(dest, fill_start, fill_n, u2p)


def _expert_weights(layer, be_ref, run_ref, nxt_ref, w_hbm, stage, sem, w_bf16):
    i = pl.program_id(0)
    changed = (i == 0) | (be_ref[i] != be_ref[jnp.maximum(i - 1, 0)])

    def copies(e, slot):
        return [pltpu.make_async_copy(w.at[layer, e], st.at[slot], sem.at[slot])
                for w, st in zip(w_hbm, stage)]

    @pl.when(changed)
    def _():
        slot = run_ref[i] % 2

        @pl.when(i == 0)
        def _():
            for cp in copies(be_ref[0], 0):
                cp.start()

        for cp in copies(be_ref[i], slot):
            cp.wait()

        @pl.when(nxt_ref[i] >= 0)
        def _():
            for cp in copies(nxt_ref[i], 1 - slot):
                cp.start()

        for st, wb in zip(stage, w_bf16):
            wb[...] = st[slot].astype(BF16)


def _experts_up_kernel(be_ref, nv_ref, run_ref, nxt_ref, x_ref, wg_hbm, wu_hbm, h_ref,
                       wg_st, wu_st, sem, wg_b, wu_b, *, layer):
    _expert_weights(layer, be_ref, run_ref, nxt_ref, (wg_hbm, wu_hbm), (wg_st, wu_st), sem, (wg_b, wu_b))

    @pl.when(pl.program_id(0) >= nv_ref[0])
    def _():
        h_ref[...] = jnp.zeros_like(h_ref)

    @pl.when(pl.program_id(0) < nv_ref[0])
    def _():
        lo, hi = _unpack_bf16_pair(x_ref[...])
        half = D_MODEL // 2
        gate = (jnp.dot(lo, wg_b[:half, :], preferred_element_type=F32)
                + jnp.dot(hi, wg_b[half:, :], preferred_element_type=F32))
        up = (jnp.dot(lo, wu_b[:half, :], preferred_element_type=F32)
              + jnp.dot(hi, wu_b[half:, :], preferred_element_type=F32))
        h_ref[...] = (gate * _sigmoid(gate) * up).astype(BF16)


def _experts_down_kernel(be_ref, nv_ref, run_ref, nxt_ref, h_ref, wd_hbm, o_ref, wd_st, sem, wd_b, *, layer):
    _expert_weights(layer, be_ref, run_ref, nxt_ref, (wd_hbm,), (wd_st,), sem, (wd_b,))

    @pl.when(pl.program_id(0) >= nv_ref[0])
    def _():
        o_ref[...] = jnp.zeros_like(o_ref)

    @pl.when(pl.program_id(0) < nv_ref[0])
    def _():
        o_ref[...] = jnp.dot(h_ref[...], wd_b[...], preferred_element_type=F32)


def _experts(layer, sched, xs, w_gate, w_up, w_down):
    rows = lambda c: pl.BlockSpec((EXPERT_TM, c), lambda i, *_: (i, 0))
    any_spec = pl.BlockSpec(memory_space=pl.ANY)
    up_shape, down_shape = (D_MODEL, D_EXPERT), (D_EXPERT, D_MODEL)
    h = pl.pallas_call(
        functools.partial(_experts_up_kernel, layer=layer),
        grid_spec=pltpu.PrefetchScalarGridSpec(
            num_scalar_prefetch=4,
            grid=(N_EXPERT_BLOCKS,),
            in_specs=[rows(D_MODEL // 2), any_spec, any_spec],
            out_specs=rows(D_EXPERT),
            scratch_shapes=[pltpu.VMEM((2,) + up_shape, F32), pltpu.VMEM((2,) + up_shape, F32),
                            pltpu.SemaphoreType.DMA((2,)),
                            pltpu.VMEM(up_shape, BF16), pltpu.VMEM(up_shape, BF16)],
        ),
        out_shape=jax.ShapeDtypeStruct((N_PAD, D_EXPERT), BF16),
        compiler_params=_cparams("arbitrary"),
        name="experts_up",
    )(*sched, xs, w_gate, w_up)
    return pl.pallas_call(
        functools.partial(_experts_down_kernel, layer=layer),
        grid_spec=pltpu.PrefetchScalarGridSpec(
            num_scalar_prefetch=4,
            grid=(N_EXPERT_BLOCKS,),
            in_specs=[rows(D_EXPERT), any_spec],
            out_specs=rows(D_MODEL),
            scratch_shapes=[pltpu.VMEM((2,) + down_shape, F32), pltpu.SemaphoreType.DMA((2,)),
                            pltpu.VMEM(down_shape, BF16)],
        ),
        out_shape=jax.ShapeDtypeStruct((N_PAD, D_MODEL), F32),
        compiler_params=_cparams("arbitrary"),
        name="experts_down",
    )(*sched, h, w_down)


def _final_kernel(dest, ys_hbm, rw_ref, x1_ref, m_ref, lng_ref, lnb_ref, *rest, emit_next):
    if emit_next:
        mn_ref, x2_ref, un_ref, buf, sem = rest
    else:
        x2_ref, buf, sem = rest
    i = pl.program_id(0)
    n = pl.num_programs(0)

    def row_dma(tile, slot, r, k):
        src = ys_hbm.at[pl.ds(dest[k * N_TOK + tile * MIX_TM + r], 1)]
        return pltpu.make_async_copy(src, buf.at[slot, k, pl.ds(r, 1)], sem.at[slot])

    def gather(tile, slot):
        def issue(r, c):
            for k in range(TOP_K):
                row_dma(tile, slot, r, k).start(priority=k)
            return c
        lax.fori_loop(0, MIX_TM, issue, 0, unroll=8)

    @pl.when(i == 0)
    def _():
        gather(0, 0)

    @pl.when(i + 1 < n)
    def _():
        gather(i + 1, (i + 1) % 2)

    slot = i % 2

    def drain(r, c):
        for k in range(TOP_K):
            pltpu.make_async_copy(ys_hbm.at[pl.ds(0, 1)], buf.at[slot, k, pl.ds(0, 1)], sem.at[slot]).wait()
        return c
    lax.fori_loop(0, MIX_TM, drain, 0, unroll=8)

    m = m_ref[0]
    g2 = m[:, 5 * D_MODEL:6 * D_MODEL]
    f = rw_ref[:, 0:1] * buf[slot, 0] + rw_ref[:, 1:2] * buf[slot, 1]
    x2 = _ln(ALPHA * x1_ref[...] + g2 * f) * lng_ref[1:2, :] + lnb_ref[1:2, :]
    x2_ref[...] = x2
    if emit_next:
        mn = mn_ref[0]
        un_ref[...] = (_ln(x2) * (1.0 + mn[:, D_MODEL:2 * D_MODEL]) + mn[:, :D_MODEL]).astype(BF16)


def _final(dest, ys, rw_t, x1, mods3, ln_g, ln_b, mods3_next=None):
    tiles_per_seq = SEQ // MIX_TM
    emit_next = mods3_next is not None
    mspec = pl.BlockSpec((1, 1, 6 * D_MODEL), lambda i, d: (i // tiles_per_seq, 0, 0))
    row = pl.BlockSpec((MIX_TM, D_MODEL), lambda i, d: (i, 0))
    in_specs = [pl.BlockSpec(memory_space=pl.ANY),
                pl.BlockSpec((MIX_TM, TOP_K), lambda i, d: (i, 0)),
                row, mspec,
                pl.BlockSpec((2, D_MODEL), lambda i, d: (0, 0)),
                pl.BlockSpec((2, D_MODEL), lambda i, d: (0, 0))]
    args = [ys, rw_t, x1, mods3, ln_g, ln_b]
    out_specs = [row]
    out_shape = [jax.ShapeDtypeStruct((N_TOK, D_MODEL), F32)]
    if emit_next:
        in_specs.append(mspec)
        args.append(mods3_next)
        out_specs.append(row)
        out_shape.append(jax.ShapeDtypeStruct((N_TOK, D_MODEL), BF16))
    return pl.pallas_call(
        functools.partial(_final_kernel, emit_next=emit_next),
        grid_spec=pltpu.PrefetchScalarGridSpec(
            num_scalar_prefetch=1,
            grid=(N_TOK // MIX_TM,),
            in_specs=in_specs,
            out_specs=out_specs,
            scratch_shapes=[pltpu.VMEM((2, TOP_K, MIX_TM, D_MODEL), F32), pltpu.SemaphoreType.DMA((2,))],
        ),
        out_shape=out_shape,
        compiler_params=_cparams("arbitrary"),
        name="final",
    )(dest, *args)


def _moe(layer, u2p, route_e, route_w, rank, cnt, w_gate, w_up, w_down):
    n_tiles = N_TOK // MIX_TM
    cnt = cnt[:, :, 0]
    counts = jnp.sum(cnt, axis=0)
    padded = (counts + EXPERT_TM - 1) // EXPERT_TM * EXPERT_TM
    pad_end = jnp.cumsum(padded)
    pad_start = pad_end - padded
    base = pad_start[None, :] + jnp.cumsum(cnt, axis=0) - cnt
    e_r = route_e.reshape(TOP_K, n_tiles, MIX_TM)
    onehot = e_r[..., None] == jnp.arange(N_EXPERTS, dtype=jnp.int32)
    dest = jnp.sum(jnp.where(onehot, base[None, :, None, :], 0), axis=-1).reshape(TOP_K, N_TOK) + rank
    dest = dest.reshape(-1).astype(jnp.int32)
    n_valid = (pad_end[-1] // EXPERT_TM).astype(jnp.int32).reshape(1)
    block_row0 = jnp.arange(N_EXPERT_BLOCKS, dtype=jnp.int32) * EXPERT_TM
    present = padded > 0
    expert_ids = jnp.arange(N_EXPERTS, dtype=jnp.int32)
    last_expert = jnp.max(jnp.where(present, expert_ids, 0))
    block_expert = jnp.minimum(
        jnp.sum(pad_end[None, :] <= block_row0[:, None], axis=1), last_expert).astype(jnp.int32)
    run_of_expert = jnp.cumsum(present) - present
    block_run = jnp.sum(
        jnp.where(expert_ids[None, :] == block_expert[:, None], run_of_expert[None, :], 0), axis=1)
    later = present[None, :] & (expert_ids[None, :] > block_expert[:, None])
    block_next = jnp.min(jnp.where(later, expert_ids[None, :], N_EXPERTS), axis=1)
    block_next = jnp.where(block_next == N_EXPERTS, -1, block_next)
    sched = (block_expert, n_valid, block_run.astype(jnp.int32), block_next.astype(jnp.int32))

    fill_start = jnp.concatenate([pad_start + counts, pad_end[-1:]]).astype(jnp.int32)
    fill_n = jnp.concatenate([padded - counts, N_PAD - pad_end[-1:]]).astype(jnp.int32)
    xs = _dispatch(u2p, dest, fill_start, fill_n)
    ys = _experts(layer, sched, xs, w_gate, w_up, w_down)
    return dest, ys, route_w.T


def _dft_tables():
    k = np.arange(SEQ, dtype=np.float64)[:, None]
    n1 = np.arange(GRID_W, dtype=np.float64)[None, :]
    coarse = 2 * np.pi * ((k * n1) % GRID_W) / GRID_W
    fine = 2 * np.pi * k * n1 / SEQ
    ac, as_ = jnp.asarray(np.cos(coarse), F32), jnp.asarray(np.sin(coarse), F32)
    bc, bs = jnp.asarray(np.cos(fine), F32), jnp.asarray(np.sin(fine), F32)
    s = SEQ ** -0.5
    cos = (ac[:, :, None] * bc[:, None, :] - as_[:, :, None] * bs[:, None, :]) * s
    sin = (as_[:, :, None] * bc[:, None, :] + ac[:, :, None] * bs[:, None, :]) * s
    cs = jnp.concatenate([cos.reshape(SEQ, SEQ), sin.reshape(SEQ, SEQ)], axis=0).astype(BF16)
    kc = np.arange(FOURIER_CH, dtype=np.float64)
    ang = 2 * np.pi * ((kc[:, None] * kc[None, :]) % FOURIER_CH) / FOURIER_CH
    cc = jnp.asarray(np.cos(ang) * FOURIER_CH ** -0.5, F32).astype(BF16)
    sc = jnp.asarray(np.sin(ang) * FOURIER_CH ** -0.5, F32).astype(BF16)
    return cs, cc, sc


def kernel(x, c, ctx, c_ctx, ada_w, ada_b, w_mix_in, rpb, conv_w, w_mix_out, w_fourier_out,
           router_w, router_b, w_gate, w_up, w_down, ln_g, ln_b):
    x2d = x.reshape(N_TOK, D_MODEL)
    c8 = jnp.concatenate(
        [c, c_ctx[None, :], jnp.zeros((SUBLANES - BATCH - 1, D_MODEL), F32)], axis=0)
    mods = _mods(c8, ada_w, ada_b)
    mods3 = [mods[l].reshape(SUBLANES, 1, 6 * D_MODEL) for l in range(DEPTH)]
    perm = lambda a: a.reshape(N_GROUPS, EXPERTS_PER_GROUP, -1).transpose(1, 0, 2).reshape(N_EXPERTS, -1)
    rwt = perm(router_w.T)
    rb = perm(router_b.reshape(N_EXPERTS, 1))
    tiles_per_seq = SEQ // PROJ_TM

    w_in = w_mix_in[0].astype(BF16)
    proj = _ln_proj(x2d, mods3[0], w_in, 0, MIX_IN_WIDTH, lambda i: i // tiles_per_seq)
    kvc = _ln_proj(ctx.reshape(BATCH * CTX_LEN, D_MODEL), mods3[0], w_in,
                   ATTN_WIDTH // PROJ_TN, 2 * ATTN_WIDTH, lambda i: BATCH)
    attn = _natten(proj, kvc, _bias_table(rpb[0]))
    x1, u2p, route_e, route_w, rank, cnt = _mix_out(
        attn, proj, conv_w[0], w_mix_out[0].astype(BF16), x2d, mods3[0], ln_g[0], ln_b[0], rwt, rb)
    dest, ys, rw_t = _moe(0, u2p, route_e, route_w, rank, cnt, w_gate, w_up, w_down)
    x2, u_next = _final(dest, ys, rw_t, x1, mods3[0], ln_g[0], ln_b[0], mods3_next=mods3[1])

    cs, cc, sc = _dft_tables()
    pq = _dft_seq(cs, u_next)
    x1, u2p, route_e, route_w, rank, cnt = _four_out(
        pq, cc, sc, w_fourier_out[0].astype(BF16), x2, mods3[1], ln_g[1], ln_b[1], rwt, rb)
    dest, ys, rw_t = _moe(1, u2p, route_e, route_w, rank, cnt, w_gate, w_up, w_down)
    (x3,) = _final(dest, ys, rw_t, x1, mods3[1], ln_g[1], ln_b[1])
    return x3.reshape(BATCH, SEQ, D_MODEL)
```

```python
import functools

import numpy as np
import jax
import jax.numpy as jnp
from jax import lax
from jax.experimental import pallas as pl
from jax.experimental.pallas import tpu as pltpu

D_MODEL = 2048
BATCH = 4
SEQ = 4096
DEPTH = 2
GRID_W = 64
GRID_ROWS = SEQ // GRID_W
CTX_LEN = 256
HEAD_DIM = 128
ATTN_WIDTH = D_MODEL // 2
N_HEADS = ATTN_WIDTH // HEAD_DIM
WIN_H = 8
WIN_W = 16
CONV_CH = D_MODEL // 2
MIX_IN_WIDTH = 3 * ATTN_WIDTH + 3 * CONV_CH
N_FOURIER_GROUPS = 4
FOURIER_CH = D_MODEL // N_FOURIER_GROUPS
N_EXPERTS = 32
N_GROUPS = 8
EXPERTS_PER_GROUP = N_EXPERTS // N_GROUPS
TOP_K = 2
D_EXPERT = D_MODEL // 2
ALPHA = (2 * DEPTH) ** 0.25
LN_EPS = 1e-6
NEG_INF = -1e30
N_TOK = BATCH * SEQ

V7X_VMEM_LIMIT_BYTES = 56 * 1024 * 1024
SUBLANES = 8
LANES = 128

PROJ_TM = 512
PROJ_TN = 1024
ATT_QROWS = 8
ATT_KROWS = 16
ATT_Q = ATT_QROWS * GRID_W
ATT_K = ATT_KROWS * GRID_W
N_QBLOCKS = GRID_ROWS // ATT_QROWS
MIX_TM = 256
DFT_TM = 1024
DFT_TN = 1024
EXPERT_TM = 256
N_ROWS = N_TOK * TOP_K
N_EXPERT_BLOCKS = N_ROWS // EXPERT_TM + N_EXPERTS
N_PAD = N_EXPERT_BLOCKS * EXPERT_TM
DISPATCH_TM = 1024
WEIGHT_DMA_CHUNKS = 4
N_FILL_SEGMENTS = N_EXPERTS + 1
ROW_CHUNKS = D_MODEL // 2 // LANES
OUT_CHUNKS = D_MODEL // LANES
MIX_BLOCK = 512
MIX_SUBTILES = MIX_BLOCK // MIX_TM
MODS_TN = 1024

F32 = jnp.float32
BF16 = jnp.bfloat16


def _cparams(*sem):
    return pltpu.CompilerParams(dimension_semantics=sem, vmem_limit_bytes=V7X_VMEM_LIMIT_BYTES)


def _ln(x):
    mu = jnp.mean(x, axis=-1, keepdims=True)
    xc = x - mu
    var = jnp.mean(xc * xc, axis=-1, keepdims=True)
    return xc * lax.rsqrt(var + LN_EPS)


def _sigmoid(x):
    return 1.0 / (1.0 + jnp.exp(-x))


def _mods_kernel(c_ref, w_ref, b_ref, o_ref):
    c = c_ref[...]
    s = (c * _sigmoid(c)).astype(BF16)
    o_ref[0] = jnp.dot(s, w_ref[0].astype(BF16), preferred_element_type=F32) + b_ref[0]


def _mods(c8, ada_w, ada_b):
    n = 6 * D_MODEL
    return pl.pallas_call(
        _mods_kernel,
        grid=(DEPTH, n // MODS_TN),
        in_specs=[
            pl.BlockSpec((SUBLANES, D_MODEL), lambda l, j: (0, 0)),
            pl.BlockSpec((1, D_MODEL, MODS_TN), lambda l, j: (l, 0, j)),
            pl.BlockSpec((1, 1, MODS_TN), lambda l, j: (l, 0, j)),
        ],
        out_specs=pl.BlockSpec((1, SUBLANES, MODS_TN), lambda l, j: (l, 0, j)),
        out_shape=jax.ShapeDtypeStruct((DEPTH, SUBLANES, n), F32),
        compiler_params=_cparams("arbitrary", "arbitrary"),
        name="mods",
    )(c8, ada_w, ada_b.reshape(DEPTH, 1, n))


def _ln_proj_kernel(x_ref, m_ref, w_ref, o_ref, u_scr):
    @pl.when(pl.program_id(1) == 0)
    def _():
        m = m_ref[0]
        y = _ln(x_ref[...])
        u_scr[...] = (y * (1.0 + m[:, D_MODEL:2 * D_MODEL]) + m[:, :D_MODEL]).astype(BF16)

    o_ref[...] = jnp.dot(u_scr[...], w_ref[...], preferred_element_type=F32).astype(o_ref.dtype)


def _ln_proj(x2d, mods3, w, col_block0, n_cols, mod_row_of_tile):
    n_rows = x2d.shape[0]
    return pl.pallas_call(
        _ln_proj_kernel,
        grid=(n_rows // PROJ_TM, n_cols // PROJ_TN),
        in_specs=[
            pl.BlockSpec((PROJ_TM, D_MODEL), lambda i, j: (i, 0)),
            pl.BlockSpec((1, 1, 6 * D_MODEL), lambda i, j: (mod_row_of_tile(i), 0, 0)),
            pl.BlockSpec((D_MODEL, PROJ_TN), lambda i, j: (0, col_block0 + j)),
        ],
        out_specs=pl.BlockSpec((PROJ_TM, PROJ_TN), lambda i, j: (i, j)),
        out_shape=jax.ShapeDtypeStruct((n_rows, n_cols), BF16),
        scratch_shapes=[pltpu.VMEM((PROJ_TM, D_MODEL), BF16)],
        compiler_params=_cparams("arbitrary", "arbitrary"),
        name="ln_proj",
    )(x2d, mods3, w)


def _bias_table_kernel(rpb_ref, o_ref):
    h = pl.program_id(0)
    qc = lax.broadcasted_iota(jnp.int32, (GRID_W, GRID_W), 0)
    kc = lax.broadcasted_iota(jnp.int32, (GRID_W, GRID_W), 1)
    col_start = jnp.clip(qc - WIN_W // 2, 0, GRID_W - WIN_W)
    col_ok = (kc >= col_start) & (kc < col_start + WIN_W)
    col_idx = jnp.clip(kc - qc, -(WIN_W - 1), WIN_W - 1) + WIN_W - 1
    n_rb = 2 * WIN_H - 1
    n_cb = 2 * WIN_W - 1
    tiles = []
    for ri in range(n_rb):
        acc = jnp.zeros((GRID_W, GRID_W), F32)
        for i in range(n_cb):
            acc = jnp.where(col_idx == i, rpb_ref[(h * n_rb + ri) * n_cb + i], acc)
        tiles.append(jnp.where(col_ok, acc, NEG_INF))
    masked = jnp.full((GRID_W, GRID_W), NEG_INF, F32)
    for typ, blk in enumerate((0, 1, N_QBLOCKS - 1)):
        ks = min(max(blk * ATT_QROWS - WIN_H // 2, 0), GRID_ROWS - ATT_KROWS)
        for a in range(ATT_QROWS):
            r = blk * ATT_QROWS + a
            rs = min(max(r - WIN_H // 2, 0), GRID_ROWS - WIN_H)
            strip = [tiles[ks + c - r + WIN_H - 1] if rs <= ks + c < rs + WIN_H else masked
                     for c in range(ATT_KROWS)]
            o_ref[0, typ, a * GRID_W:(a + 1) * GRID_W, :] = jnp.concatenate(strip, axis=1)


def _bias_table(rpb_l):
    flat = rpb_l.reshape(-1).astype(F32)
    return pl.pallas_call(
        _bias_table_kernel,
        grid=(N_HEADS,),
        in_specs=[pl.BlockSpec(memory_space=pltpu.SMEM)],
        out_specs=pl.BlockSpec((1, 3, ATT_Q, ATT_K), lambda h: (h, 0, 0, 0)),
        out_shape=jax.ShapeDtypeStruct((N_HEADS, 3, ATT_Q, ATT_K), F32),
        compiler_params=_cparams("arbitrary"),
        name="bias_table",
    )(flat)


def _natten_kernel(q_ref, k_ref, v_ref, kc_ref, vc_ref, bias_ref, o_ref):
    scale = HEAD_DIM ** -0.5
    kc = kc_ref[...]
    vc = vc_ref[...]
    nt = (((1,), (1,)), ((), ()))

    def body(j, carry):
        typ = jnp.where(j == 0, 0, jnp.where(j == N_QBLOCKS - 1, 2, 1))
        q0 = pl.multiple_of(j * ATT_Q, ATT_Q)
        k0 = pl.multiple_of(
            jnp.clip(j * ATT_QROWS - WIN_H // 2, 0, GRID_ROWS - ATT_KROWS) * GRID_W, 4 * GRID_W)
        q = q_ref[pl.ds(q0, ATT_Q), :]
        kw = k_ref[pl.ds(k0, ATT_K), :]
        vw = v_ref[pl.ds(k0, ATT_K), :]
        s_loc = lax.dot_general(q, kw, nt, preferred_element_type=F32) * scale + bias_ref[typ]
        s_ctx = lax.dot_general(q, kc, nt, preferred_element_type=F32) * scale
        m = jnp.maximum(jnp.max(s_loc, axis=-1, keepdims=True), jnp.max(s_ctx, axis=-1, keepdims=True))
        p_loc = jnp.exp(s_loc - m)
        p_ctx = jnp.exp(s_ctx - m)
        denom = jnp.sum(p_loc, axis=-1, keepdims=True) + jnp.sum(p_ctx, axis=-1, keepdims=True)
        o = (jnp.dot(p_loc.astype(BF16), vw, preferred_element_type=F32)
             + jnp.dot(p_ctx.astype(BF16), vc, preferred_element_type=F32))
        o_ref[pl.ds(q0, ATT_Q), :] = (o / denom).astype(o_ref.dtype)
        return carry

    lax.fori_loop(0, N_QBLOCKS, body, 0)


def _natten(proj, kvc, bias):
    blk = lambda off: pl.BlockSpec((SEQ, HEAD_DIM), lambda b, h: (b, off + h))
    cblk = lambda off: pl.BlockSpec((CTX_LEN, HEAD_DIM), lambda b, h: (b, off + h))
    return pl.pallas_call(
        _natten_kernel,
        grid=(BATCH, N_HEADS),
        in_specs=[blk(0), blk(N_HEADS), blk(2 * N_HEADS), cblk(0), cblk(N_HEADS),
                  pl.BlockSpec((None, 3, ATT_Q, ATT_K), lambda b, h: (h, 0, 0, 0))],
        out_specs=pl.BlockSpec((SEQ, HEAD_DIM), lambda b, h: (b, h)),
        out_shape=jax.ShapeDtypeStruct((N_TOK, ATTN_WIDTH), BF16),
        compiler_params=_cparams("arbitrary", "arbitrary"),
        name="natten",
    )(proj, proj, proj, kvc, kvc, bias)


def _pack_bf16_pair(lo, hi):
    return pltpu.pack_elementwise([lo, hi], packed_dtype=BF16)


def _unpack_bf16_pair(w):
    unpack = lambda i: pltpu.unpack_elementwise(w, index=i, packed_dtype=BF16, unpacked_dtype=F32)
    return unpack(0), unpack(1)


def _store_token_tiles(ref, vals):
    tm, n = vals.shape[0], vals.shape[1] // LANES
    for c in range(n):
        ref[pl.ds(c, tm, stride=n), :] = vals[:, c * LANES:(c + 1) * LANES]


def _load_token_tiles(ref, tm):
    n = ref.shape[0] // tm
    return jnp.concatenate([ref[pl.ds(c, tm, stride=n), :] for c in range(n)], axis=1)


def _epilogue(y, s, x_ref, m_ref, lng_ref, lnb_ref, rwh_ref, rwl_ref, rb_ref,
              x1_ref, u2_ref, re_ref, rw_ref, rk_ref, cnt_ref):
    tm = MIX_TM
    rows = slice(s * tm, (s + 1) * tm)
    m = m_ref[0]
    g1 = m[:, 2 * D_MODEL:3 * D_MODEL]
    sh2 = m[:, 3 * D_MODEL:4 * D_MODEL]
    sc2 = m[:, 4 * D_MODEL:5 * D_MODEL]
    x1 = _ln(ALPHA * x_ref[rows, :] + g1 * y) * lng_ref[0:1, :] + lnb_ref[0:1, :]
    x1_ref[rows, :] = x1
    u2 = _ln(x1) * (1.0 + sc2) + sh2
    _store_token_tiles(u2_ref.at[pl.ds(s * tm * ROW_CHUNKS, tm * ROW_CHUNKS)],
                       _pack_bf16_pair(u2[:, :D_MODEL // 2], u2[:, D_MODEL // 2:]))

    u_hi = u2.astype(BF16)
    u_lo = (u2 - u_hi.astype(F32)).astype(BF16)
    logits = (jnp.dot(u_hi, rwh_ref[...], preferred_element_type=F32)
              + jnp.dot(u_lo, rwh_ref[...], preferred_element_type=F32)
              + jnp.dot(u_hi, rwl_ref[...], preferred_element_type=F32))
    aff = _sigmoid(logits.T[:N_EXPERTS, :])
    biased = aff + rb_ref[...]
    a = [biased[l * N_GROUPS:(l + 1) * N_GROUPS, :] for l in range(EXPERTS_PER_GROUP)]
    f = [aff[l * N_GROUPS:(l + 1) * N_GROUPS, :] for l in range(EXPERTS_PER_GROUP)]
    hi01, lo01 = jnp.maximum(a[0], a[1]), jnp.minimum(a[0], a[1])
    hi23, lo23 = jnp.maximum(a[2], a[3]), jnp.minimum(a[2], a[3])
    top1 = jnp.maximum(hi01, hi23)
    top2 = jnp.maximum(jnp.minimum(hi01, hi23), jnp.maximum(lo01, lo23))
    gscore = top1 + top2
    gi = lax.broadcasted_iota(jnp.int32, (N_GROUPS, tm), 0)
    gmax = jnp.max(gscore, axis=0, keepdims=True)
    group = jnp.min(jnp.where(gscore == gmax, gi, N_GROUPS), axis=0, keepdims=True)
    sel = gi == group
    b = [jnp.sum(jnp.where(sel, a[l], 0.0), axis=0, keepdims=True) for l in range(EXPERTS_PER_GROUP)]
    c = [jnp.sum(jnp.where(sel, f[l], 0.0), axis=0, keepdims=True) for l in range(EXPERTS_PER_GROUP)]

    def first_argmax(vals):
        mx = functools.reduce(jnp.maximum, vals)
        idx = jnp.full(mx.shape, len(vals) - 1, jnp.int32)
        for l in range(len(vals) - 2, -1, -1):
            idx = jnp.where(vals[l] == mx, l, idx)
        return idx

    i1 = first_argmax(b)
    i2 = first_argmax([jnp.where(i1 == l, -jnp.inf, b[l]) for l in range(EXPERTS_PER_GROUP)])
    pick = lambda idx: functools.reduce(
        lambda acc, l: jnp.where(idx == l, c[l], acc), range(1, EXPERTS_PER_GROUP), c[0])
    w1, w2 = pick(i1), pick(i2)
    wsum = w1 + w2
    e1 = group * EXPERTS_PER_GROUP + i1
    e2 = group * EXPERTS_PER_GROUP + i2
    re_ref[0:1, rows] = e1
    re_ref[1:2, rows] = e2
    rw_ref[0:1, rows] = w1 / wsum
    rw_ref[1:2, rows] = w2 / wsum

    ei = lax.broadcasted_iota(jnp.int32, (N_EXPERTS, tm), 0)
    oh1 = ei == e1
    oh2 = ei == e2
    oh = jnp.where(oh1 | oh2, 1.0, 0.0)
    t_row = lax.broadcasted_iota(jnp.int32, (tm, tm), 0)
    t_col = lax.broadcasted_iota(jnp.int32, (tm, tm), 1)
    before = jnp.where(t_row < t_col, 1.0, 0.0).astype(BF16)
    prefix = jnp.dot(oh.astype(BF16), before, preferred_element_type=F32)
    rk_ref[0:1, rows] = jnp.sum(jnp.where(oh1, prefix, 0.0), axis=0, keepdims=True).astype(jnp.int32)
    rk_ref[1:2, rows] = jnp.sum(jnp.where(oh2, prefix, 0.0), axis=0, keepdims=True).astype(jnp.int32)
    cnt = jnp.sum(oh, axis=1, keepdims=True)
    cnt_ref[s] = jnp.broadcast_to(cnt, (N_EXPERTS, LANES)).astype(jnp.int32)


def _resident(shape):
    return pl.BlockSpec(shape, lambda i: (0,) * len(shape), pipeline_mode=pl.Buffered(1))


def _epilogue_specs():
    blocks_per_seq = SEQ // MIX_BLOCK
    n_tiles = N_TOK // MIX_TM
    in_specs = [
        pl.BlockSpec((MIX_BLOCK, D_MODEL), lambda i: (i, 0)),
        pl.BlockSpec((1, 1, 6 * D_MODEL), lambda i: (i // blocks_per_seq, 0, 0)),
        _resident((2, D_MODEL)),
        _resident((2, D_MODEL)),
        _resident((D_MODEL, LANES)),
        _resident((D_MODEL, LANES)),
        _resident((N_EXPERTS, 1)),
    ]
    out_specs = [
        pl.BlockSpec((MIX_BLOCK, D_MODEL), lambda i: (i, 0)),
        pl.BlockSpec((MIX_BLOCK * ROW_CHUNKS, LANES), lambda i: (i, 0)),
        pl.BlockSpec((TOP_K, MIX_BLOCK), lambda i: (0, i)),
        pl.BlockSpec((TOP_K, MIX_BLOCK), lambda i: (0, i)),
        pl.BlockSpec((TOP_K, MIX_BLOCK), lambda i: (0, i)),
        pl.BlockSpec((MIX_SUBTILES, N_EXPERTS, LANES), lambda i: (i, 0, 0)),
    ]
    out_shape = [
        jax.ShapeDtypeStruct((N_TOK, D_MODEL), F32),
        jax.ShapeDtypeStruct((N_TOK * ROW_CHUNKS, LANES), jnp.uint32),
        jax.ShapeDtypeStruct((TOP_K, N_TOK), jnp.int32),
        jax.ShapeDtypeStruct((TOP_K, N_TOK), F32),
        jax.ShapeDtypeStruct((TOP_K, N_TOK), jnp.int32),
        jax.ShapeDtypeStruct((n_tiles, N_EXPERTS, LANES), jnp.int32),
    ]
    return in_specs, out_specs, out_shape


def _mix_out_kernel(attn_ref, gb_ref, gc_ref, xin_ref, gcp_ref, xinp_ref, gcn_ref, xinn_ref,
                    cw_ref, wo_ref, *epilogue_refs):
    i = pl.program_id(0)
    blocks_per_seq = SEQ // MIX_BLOCK
    z = gc_ref[...].astype(F32) * xin_ref[...].astype(F32)
    halo = 2 * SUBLANES
    zp_row = (gcp_ref[...].astype(F32) * xinp_ref[...].astype(F32))[halo - 1:halo, :]
    zn_row = (gcn_ref[...].astype(F32) * xinn_ref[...].astype(F32))[0:1, :]
    zp_row = jnp.where(i % blocks_per_seq == 0, 0.0, zp_row)
    zn_row = jnp.where(i % blocks_per_seq == blocks_per_seq - 1, 0.0, zn_row)
    row = lax.broadcasted_iota(jnp.int32, (MIX_BLOCK, 1), 0)
    z_prev = jnp.where(row == 0, zp_row, pltpu.roll(z, 1, axis=0))
    z_next = jnp.where(row == MIX_BLOCK - 1, zn_row, pltpu.roll(z, MIX_BLOCK - 1, axis=0))
    conv = cw_ref[0:1, :] * z_prev + cw_ref[1:2, :] * z + cw_ref[2:3, :] * z_next
    gated = (gb_ref[...].astype(F32) * conv).astype(BF16)
    for s in range(MIX_SUBTILES):
        rows = slice(s * MIX_TM, (s + 1) * MIX_TM)
        y = (jnp.dot(attn_ref[rows, :], wo_ref[:ATTN_WIDTH, :], preferred_element_type=F32)
             + jnp.dot(gated[rows, :], wo_ref[ATTN_WIDTH:, :], preferred_element_type=F32))
        _epilogue(y, s, *epilogue_refs)


def _mix_out(attn, proj, conv_w, w_out, x2d, mods3, ln_g, ln_b, rw_hi, rw_lo, rb):
    halo = 2 * SUBLANES
    hb = MIX_BLOCK // halo
    n_halo_blocks = N_TOK // halo
    cblk = lambda off: pl.BlockSpec((MIX_BLOCK, CONV_CH), lambda i: (i, off))
    prev = lambda off: pl.BlockSpec((halo, CONV_CH), lambda i: (jnp.maximum(i * hb - 1, 0), off))
    nxt = lambda off: pl.BlockSpec(
        (halo, CONV_CH), lambda i: (jnp.minimum((i + 1) * hb, n_halo_blocks - 1), off))
    ep_in, out_specs, out_shape = _epilogue_specs()
    return pl.pallas_call(
        _mix_out_kernel,
        grid=(N_TOK // MIX_BLOCK,),
        in_specs=[pl.BlockSpec((MIX_BLOCK, ATTN_WIDTH), lambda i: (i, 0)),
                  cblk(3), cblk(4), cblk(5), prev(4), prev(5), nxt(4), nxt(5),
                  _resident((3, CONV_CH)), _resident((D_MODEL, D_MODEL))] + ep_in,
        out_specs=out_specs,
        out_shape=out_shape,
        compiler_params=_cparams("arbitrary"),
        name="mix_out",
    )(attn, proj, proj, proj, proj, proj, proj, proj, conv_w, w_out, x2d, mods3, ln_g, ln_b,
      rw_hi, rw_lo, rb)


def _dft_seq_kernel(cs_ref, u_ref, o_ref):
    o_ref[...] = jnp.dot(cs_ref[...], u_ref[...], preferred_element_type=F32).astype(o_ref.dtype)


def _dft_seq(cs, u):
    m = 2 * SEQ
    return pl.pallas_call(
        _dft_seq_kernel,
        grid=(BATCH, D_MODEL // DFT_TN, m // DFT_TM),
        in_specs=[pl.BlockSpec((DFT_TM, SEQ), lambda b, n, i: (i, 0)),
                  pl.BlockSpec((SEQ, DFT_TN), lambda b, n, i: (b, n))],
        out_specs=pl.BlockSpec((DFT_TM, DFT_TN), lambda b, n, i: (b * (m // DFT_TM) + i, n)),
        out_shape=jax.ShapeDtypeStruct((BATCH * m, D_MODEL), BF16),
        compiler_params=_cparams("arbitrary", "arbitrary", "arbitrary"),
        name="dft_seq",
    )(cs, u)


def _four_out_kernel(p_ref, q_ref, cc_ref, sc_ref, wf_ref, *epilogue_refs):
    for s in range(MIX_SUBTILES):
        rows = slice(s * MIX_TM, (s + 1) * MIX_TM)
        y = jnp.zeros((MIX_TM, D_MODEL), F32)
        for g in range(N_FOURIER_GROUPS):
            cols = slice(g * FOURIER_CH, (g + 1) * FOURIER_CH)
            fg = (jnp.dot(p_ref[rows, cols], cc_ref[...], preferred_element_type=F32)
                  - jnp.dot(q_ref[rows, cols], sc_ref[...], preferred_element_type=F32))
            y = y + jnp.dot(fg.astype(BF16), wf_ref[cols, :], preferred_element_type=F32)
        _epilogue(y, s, *epilogue_refs)


def _four_out(pq, cc, sc, wf, x2d, mods3, ln_g, ln_b, rw_hi, rw_lo, rb):
    blocks_per_seq = SEQ // MIX_BLOCK
    ep_in, out_specs, out_shape = _epilogue_specs()
    prow = lambda i: (i // blocks_per_seq) * 2 * blocks_per_seq + i % blocks_per_seq
    return pl.pallas_call(
        _four_out_kernel,
        grid=(N_TOK // MIX_BLOCK,),
        in_specs=[pl.BlockSpec((MIX_BLOCK, D_MODEL), lambda i: (prow(i), 0)),
                  pl.BlockSpec((MIX_BLOCK, D_MODEL), lambda i: (prow(i) + blocks_per_seq, 0)),
                  _resident((FOURIER_CH, FOURIER_CH)), _resident((FOURIER_CH, FOURIER_CH)),
                  _resident((D_MODEL, D_MODEL))] + ep_in,
        out_specs=out_specs,
        out_shape=out_shape,
        compiler_params=_cparams("arbitrary"),
        name="four_out",
    )(pq, pq, cc, sc, wf, x2d, mods3, ln_g, ln_b, rw_hi, rw_lo, rb)


def _dispatch_kernel(dest, fill_start, fill_n, u_ref, out_hbm, sem):
    i = pl.program_id(0)
    base = i * DISPATCH_TM

    def row_dma(r, d):
        src = u_ref.at[pl.ds(pl.multiple_of(r * ROW_CHUNKS, ROW_CHUNKS), ROW_CHUNKS)]
        dst = out_hbm.at[pl.ds(pl.multiple_of(d * ROW_CHUNKS, ROW_CHUNKS), ROW_CHUNKS)]
        return pltpu.make_async_copy(src, dst, sem)

    def issue(r, c):
        for k in range(TOP_K):
            row_dma(r, dest[k * N_TOK + base + r]).start(priority=k)
        return c

    def drain(r, c):
        for k in range(TOP_K):
            row_dma(0, 0).wait()
        return c

    lax.fori_loop(0, DISPATCH_TM, issue, 0, unroll=8)
    lax.fori_loop(0, DISPATCH_TM, drain, 0, unroll=8)

    @pl.when(i == pl.num_programs(0) - 1)
    def _():
        def per_segment(e, c):
            lax.fori_loop(0, fill_n[e], lambda r, c2: (row_dma(0, fill_start[e] + r).start(), c2)[1], 0)
            lax.fori_loop(0, fill_n[e], lambda r, c2: (row_dma(0, 0).wait(), c2)[1], 0)
            return c

        lax.fori_loop(0, N_FILL_SEGMENTS, per_segment, 0)


def _dispatch(u2p, dest, fill_start, fill_n):
    return pl.pallas_call(
        _dispatch_kernel,
        grid_spec=pltpu.PrefetchScalarGridSpec(
            num_scalar_prefetch=3,
            grid=(N_TOK // DISPATCH_TM,),
            in_specs=[pl.BlockSpec((DISPATCH_TM * ROW_CHUNKS, LANES), lambda i, *_: (i, 0))],
            out_specs=pl.BlockSpec(memory_space=pl.ANY),
            scratch_shapes=[pltpu.SemaphoreType.DMA(())],
        ),
        out_shape=jax.ShapeDtypeStruct((N_PAD * ROW_CHUNKS, LANES), jnp.uint32),
        compiler_params=_cparams("arbitrary"),
        name="dispatch",
    )(dest, fill_start, fill_n, u2p)


def _expert_weights(layer, be_ref, run_ref, nxt_ref, w_hbm, stage, sem, w_bf16):
    i = pl.program_id(0)
    changed = (i == 0) | (be_ref[i] != be_ref[jnp.maximum(i - 1, 0)])

    def copies(e, slot):
        out = []
        for w, st in zip(w_hbm, stage):
            rows = st.shape[1] // WEIGHT_DMA_CHUNKS
            for c in range(WEIGHT_DMA_CHUNKS):
                rs = pl.ds(c * rows, rows)
                out.append(pltpu.make_async_copy(w.at[layer, e, rs], st.at[slot, rs], sem.at[slot]))
        return out

    @pl.when(changed)
    def _():
        slot = run_ref[i] % 2

        @pl.when(i == 0)
        def _():
            for cp in copies(be_ref[0], 0):
                cp.start(priority=1)

        for cp in copies(be_ref[i], slot):
            cp.wait()

        @pl.when(nxt_ref[i] >= 0)
        def _():
            for cp in copies(nxt_ref[i], 1 - slot):
                cp.start(priority=1)

        for st, wb in zip(stage, w_bf16):
            wb[...] = st[slot].astype(BF16)


def _experts_up_kernel(be_ref, nv_ref, run_ref, nxt_ref, x_ref, wg_hbm, wu_hbm, h_ref,
                       wg_st, wu_st, sem, wg_b, wu_b, *, layer):
    _expert_weights(layer, be_ref, run_ref, nxt_ref, (wg_hbm, wu_hbm), (wg_st, wu_st), sem, (wg_b, wu_b))

    @pl.when(pl.program_id(0) >= nv_ref[0])
    def _():
        h_ref[...] = jnp.zeros_like(h_ref)

    @pl.when(pl.program_id(0) < nv_ref[0])
    def _():
        lo, hi = _unpack_bf16_pair(_load_token_tiles(x_ref, EXPERT_TM))
        lo, hi = lo.astype(BF16), hi.astype(BF16)
        half = D_MODEL // 2
        gate = (jnp.dot(lo, wg_b[:half, :], preferred_element_type=F32)
                + jnp.dot(hi, wg_b[half:, :], preferred_element_type=F32))
        up = (jnp.dot(lo, wu_b[:half, :], preferred_element_type=F32)
              + jnp.dot(hi, wu_b[half:, :], preferred_element_type=F32))
        h_ref[...] = (gate * _sigmoid(gate) * up).astype(BF16)


def _experts_down_kernel(be_ref, nv_ref, run_ref, nxt_ref, h_ref, wd_hbm, o_ref, wd_st, sem, wd_b, *, layer):
    _expert_weights(layer, be_ref, run_ref, nxt_ref, (wd_hbm,), (wd_st,), sem, (wd_b,))

    @pl.when(pl.program_id(0) >= nv_ref[0])
    def _():
        o_ref[...] = jnp.zeros_like(o_ref)

    @pl.when(pl.program_id(0) < nv_ref[0])
    def _():
        _store_token_tiles(o_ref, jnp.dot(h_ref[...], wd_b[...], preferred_element_type=F32))


def _experts(layer, sched, xs, w_gate, w_up, w_down):
    rows = lambda c: pl.BlockSpec((EXPERT_TM, c), lambda i, *_: (i, 0))
    tiles = pl.BlockSpec((EXPERT_TM * ROW_CHUNKS, LANES), lambda i, *_: (i, 0))
    any_spec = pl.BlockSpec(memory_space=pl.ANY)
    up_shape, down_shape = (D_MODEL, D_EXPERT), (D_EXPERT, D_MODEL)
    h = pl.pallas_call(
        functools.partial(_experts_up_kernel, layer=layer),
        grid_spec=pltpu.PrefetchScalarGridSpec(
            num_scalar_prefetch=4,
            grid=(N_EXPERT_BLOCKS,),
            in_specs=[tiles, any_spec, any_spec],
            out_specs=rows(D_EXPERT),
            scratch_shapes=[pltpu.VMEM((2,) + up_shape, F32), pltpu.VMEM((2,) + up_shape, F32),
                            pltpu.SemaphoreType.DMA((2,)),
                            pltpu.VMEM(up_shape, BF16), pltpu.VMEM(up_shape, BF16)],
        ),
        out_shape=jax.ShapeDtypeStruct((N_PAD, D_EXPERT), BF16),
        compiler_params=_cparams("arbitrary"),
        name="experts_up",
    )(*sched, xs, w_gate, w_up)
    return pl.pallas_call(
        functools.partial(_experts_down_kernel, layer=layer),
        grid_spec=pltpu.PrefetchScalarGridSpec(
            num_scalar_prefetch=4,
            grid=(N_EXPERT_BLOCKS,),
            in_specs=[rows(D_EXPERT), any_spec],
            out_specs=pl.BlockSpec((EXPERT_TM * OUT_CHUNKS, LANES), lambda i, *_: (i, 0)),
            scratch_shapes=[pltpu.VMEM((2,) + down_shape, F32), pltpu.SemaphoreType.DMA((2,)),
                            pltpu.VMEM(down_shape, BF16)],
        ),
        out_shape=jax.ShapeDtypeStruct((N_PAD * OUT_CHUNKS, LANES), F32),
        compiler_params=_cparams("arbitrary"),
        name="experts_down",
    )(*sched, h, w_down)


def _final_kernel(dest, ys_hbm, rw_ref, x1_ref, m_ref, lng_ref, lnb_ref, *rest, emit_next):
    if emit_next:
        mn_ref, x2_ref, un_ref, buf, sem = rest
    else:
        x2_ref, buf, sem = rest
    i = pl.program_id(0)
    n = pl.num_programs(0)

    def row_dma(tile, slot, r, k):
        d = dest[k * N_TOK + tile * MIX_TM + r]
        src = ys_hbm.at[pl.ds(pl.multiple_of(d * OUT_CHUNKS, OUT_CHUNKS), OUT_CHUNKS)]
        dst = buf.at[slot, k, pl.ds(pl.multiple_of(r * OUT_CHUNKS, OUT_CHUNKS), OUT_CHUNKS)]
        return pltpu.make_async_copy(src, dst, sem.at[slot])

    def gather(tile, slot):
        def issue(r, c):
            for k in range(TOP_K):
                row_dma(tile, slot, r, k).start(priority=k)
            return c
        lax.fori_loop(0, MIX_TM, issue, 0, unroll=8)

    @pl.when(i == 0)
    def _():
        gather(0, 0)

    @pl.when(i + 1 < n)
    def _():
        gather(i + 1, (i + 1) % 2)

    slot = i % 2

    def drain(r, c):
        for k in range(TOP_K):
            row_dma(0, slot, 0, k).wait()
        return c
    lax.fori_loop(0, MIX_TM, drain, 0, unroll=8)

    expert_rows = lambda k: _load_token_tiles(buf.at[slot, k], MIX_TM)
    m = m_ref[0]
    g2 = m[:, 5 * D_MODEL:6 * D_MODEL]
    f = rw_ref[:, 0:1] * expert_rows(0) + rw_ref[:, 1:2] * expert_rows(1)
    x2 = _ln(ALPHA * x1_ref[...] + g2 * f) * lng_ref[1:2, :] + lnb_ref[1:2, :]
    x2_ref[...] = x2
    if emit_next:
        mn = mn_ref[0]
        un_ref[...] = (_ln(x2) * (1.0 + mn[:, D_MODEL:2 * D_MODEL]) + mn[:, :D_MODEL]).astype(BF16)


def _final(dest, ys, rw_t, x1, mods3, ln_g, ln_b, mods3_next=None):
    tiles_per_seq = SEQ // MIX_TM
    emit_next = mods3_next is not None
    mspec = pl.BlockSpec((1, 1, 6 * D_MODEL), lambda i, d: (i // tiles_per_seq, 0, 0))
    row = pl.BlockSpec((MIX_TM, D_MODEL), lambda i, d: (i, 0))
    in_specs = [pl.BlockSpec(memory_space=pl.ANY),
                pl.BlockSpec((MIX_TM, TOP_K), lambda i, d: (i, 0)),
                row, mspec,
                pl.BlockSpec((2, D_MODEL), lambda i, d: (0, 0)),
                pl.BlockSpec((2, D_MODEL), lambda i, d: (0, 0))]
    args = [ys, rw_t, x1, mods3, ln_g, ln_b]
    out_specs = [row]
    out_shape = [jax.ShapeDtypeStruct((N_TOK, D_MODEL), F32)]
    if emit_next:
        in_specs.append(mspec)
        args.append(mods3_next)
        out_specs.append(row)
        out_shape.append(jax.ShapeDtypeStruct((N_TOK, D_MODEL), BF16))
    return pl.pallas_call(
        functools.partial(_final_kernel, emit_next=emit_next),
        grid_spec=pltpu.PrefetchScalarGridSpec(
            num_scalar_prefetch=1,
            grid=(N_TOK // MIX_TM,),
            in_specs=in_specs,
            out_specs=out_specs,
            scratch_shapes=[pltpu.VMEM((2, TOP_K, MIX_TM * OUT_CHUNKS, LANES), F32),
                            pltpu.SemaphoreType.DMA((2,))],
        ),
        out_shape=out_shape,
        compiler_params=_cparams("arbitrary"),
        name="final",
    )(dest, *args)


def _moe(layer, u2p, route_e, route_w, rank, cnt, w_gate, w_up, w_down):
    n_tiles = N_TOK // MIX_TM
    cnt = cnt[:, :, 0]
    counts = jnp.sum(cnt, axis=0)
    padded = (counts + EXPERT_TM - 1) // EXPERT_TM * EXPERT_TM
    pad_end = jnp.cumsum(padded)
    pad_start = pad_end - padded
    base = pad_start[None, :] + jnp.cumsum(cnt, axis=0) - cnt
    e_r = route_e.reshape(TOP_K, n_tiles, MIX_TM)
    onehot = e_r[..., None] == jnp.arange(N_EXPERTS, dtype=jnp.int32)
    dest = jnp.sum(jnp.where(onehot, base[None, :, None, :], 0), axis=-1).reshape(TOP_K, N_TOK) + rank
    dest = dest.reshape(-1).astype(jnp.int32)
    n_valid = (pad_end[-1] // EXPERT_TM).astype(jnp.int32).reshape(1)
    block_row0 = jnp.arange(N_EXPERT_BLOCKS, dtype=jnp.int32) * EXPERT_TM
    present = padded > 0
    expert_ids = jnp.arange(N_EXPERTS, dtype=jnp.int32)
    last_expert = jnp.max(jnp.where(present, expert_ids, 0))
    block_expert = jnp.minimum(
        jnp.sum(pad_end[None, :] <= block_row0[:, None], axis=1), last_expert).astype(jnp.int32)
    run_of_expert = jnp.cumsum(present) - present
    block_run = jnp.sum(
        jnp.where(expert_ids[None, :] == block_expert[:, None], run_of_expert[None, :], 0), axis=1)
    later = present[None, :] & (expert_ids[None, :] > block_expert[:, None])
    block_next = jnp.min(jnp.where(later, expert_ids[None, :], N_EXPERTS), axis=1)
    block_next = jnp.where(block_next == N_EXPERTS, -1, block_next)
    sched = (block_expert, n_valid, block_run.astype(jnp.int32), block_next.astype(jnp.int32))

    fill_start = jnp.concatenate([pad_start + counts, pad_end[-1:]]).astype(jnp.int32)
    fill_n = jnp.concatenate([padded - counts, N_PAD - pad_end[-1:]]).astype(jnp.int32)
    xs = _dispatch(u2p, dest, fill_start, fill_n)
    ys = _experts(layer, sched, xs, w_gate, w_up, w_down)
    return dest, ys, route_w.T


def _dft_tables():
    k = np.arange(SEQ, dtype=np.float64)[:, None]
    n1 = np.arange(GRID_W, dtype=np.float64)[None, :]
    coarse = 2 * np.pi * ((k * n1) % GRID_W) / GRID_W
    fine = 2 * np.pi * k * n1 / SEQ
    ac, as_ = jnp.asarray(np.cos(coarse), F32), jnp.asarray(np.sin(coarse), F32)
    bc, bs = jnp.asarray(np.cos(fine), F32), jnp.asarray(np.sin(fine), F32)
    s = SEQ ** -0.5
    cos = (ac[:, :, None] * bc[:, None, :] - as_[:, :, None] * bs[:, None, :]) * s
    sin = (as_[:, :, None] * bc[:, None, :] + ac[:, :, None] * bs[:, None, :]) * s
    cs = jnp.concatenate([cos.reshape(SEQ, SEQ), sin.reshape(SEQ, SEQ)], axis=0).astype(BF16)
    kc = np.arange(FOURIER_CH, dtype=np.float64)
    ang = 2 * np.pi * ((kc[:, None] * kc[None, :]) % FOURIER_CH) / FOURIER_CH
    cc = jnp.asarray(np.cos(ang) * FOURIER_CH ** -0.5, F32).astype(BF16)
    sc = jnp.asarray(np.sin(ang) * FOURIER_CH ** -0.5, F32).astype(BF16)
    return cs, cc, sc


def kernel(x, c, ctx, c_ctx, ada_w, ada_b, w_mix_in, rpb, conv_w, w_mix_out, w_fourier_out,
           router_w, router_b, w_gate, w_up, w_down, ln_g, ln_b):
    x2d = x.reshape(N_TOK, D_MODEL)
    c8 = jnp.concatenate(
        [c, c_ctx[None, :], jnp.zeros((SUBLANES - BATCH - 1, D_MODEL), F32)], axis=0)
    mods = _mods(c8, ada_w, ada_b)
    mods3 = [mods[l].reshape(SUBLANES, 1, 6 * D_MODEL) for l in range(DEPTH)]
    perm = lambda a: a.reshape(N_GROUPS, EXPERTS_PER_GROUP, -1).transpose(1, 0, 2).reshape(N_EXPERTS, -1)
    rw = jnp.pad(perm(router_w.T).T, ((0, 0), (0, LANES - N_EXPERTS)))
    rw_hi = rw.astype(BF16)
    rw_lo = (rw - rw_hi.astype(F32)).astype(BF16)
    rb = perm(router_b.reshape(N_EXPERTS, 1))
    tiles_per_seq = SEQ // PROJ_TM

    w_in = w_mix_in[0].astype(BF16)
    proj = _ln_proj(x2d, mods3[0], w_in, 0, MIX_IN_WIDTH, lambda i: i // tiles_per_seq)
    kvc = _ln_proj(ctx.reshape(BATCH * CTX_LEN, D_MODEL), mods3[0], w_in,
                   ATTN_WIDTH // PROJ_TN, 2 * ATTN_WIDTH, lambda i: BATCH)
    attn = _natten(proj, kvc, _bias_table(rpb[0]))
    x1, u2p, route_e, route_w, rank, cnt = _mix_out(
        attn, proj, conv_w[0], w_mix_out[0].astype(BF16), x2d, mods3[0], ln_g[0], ln_b[0],
        rw_hi, rw_lo, rb)
    dest, ys, rw_t = _moe(0, u2p, route_e, route_w, rank, cnt, w_gate, w_up, w_down)
    x2, u_next = _final(dest, ys, rw_t, x1, mods3[0], ln_g[0], ln_b[0], mods3_next=mods3[1])

    cs, cc, sc = _dft_tables()
    pq = _dft_seq(cs, u_next)
    x1, u2p, route_e, route_w, rank, cnt = _four_out(
        pq, cc, sc, w_fourier_out[0].astype(BF16), x2, mods3[1], ln_g[1], ln_b[1],
        rw_hi, rw_lo, rb)
    dest, ys, rw_t = _moe(1, u2p, route_e, route_w, rank, cnt, w_gate, w_up, w_down)
    (x3,) = _final(dest, ys, rw_t, x1, mods3[1], ln_g[1], ln_b[1])
    return x3.reshape(BATCH, SEQ, D_MODEL)
```

```python
import functools

import numpy as np
import jax
import jax.numpy as jnp
from jax import lax
from jax.experimental import pallas as pl
from jax.experimental.pallas import tpu as pltpu

D_MODEL = 2048
BATCH = 4
SEQ = 4096
DEPTH = 2
GRID_W = 64
GRID_ROWS = SEQ // GRID_W
CTX_LEN = 256
HEAD_DIM = 128
ATTN_WIDTH = D_MODEL // 2
N_HEADS = ATTN_WIDTH // HEAD_DIM
WIN_H = 8
WIN_W = 16
CONV_CH = D_MODEL // 2
MIX_IN_WIDTH = 3 * ATTN_WIDTH + 3 * CONV_CH
N_FOURIER_GROUPS = 4
FOURIER_CH = D_MODEL // N_FOURIER_GROUPS
N_EXPERTS = 32
N_GROUPS = 8
EXPERTS_PER_GROUP = N_EXPERTS // N_GROUPS
TOP_K = 2
D_EXPERT = D_MODEL // 2
ALPHA = (2 * DEPTH) ** 0.25
LN_EPS = 1e-6
NEG_INF = -1e30
N_TOK = BATCH * SEQ

V7X_VMEM_LIMIT_BYTES = 56 * 1024 * 1024
SUBLANES = 8
LANES = 128

PROJ_TM = 512
PROJ_TN = 1024
ATT_QROWS = 8
ATT_KROWS = 16
ATT_Q = ATT_QROWS * GRID_W
ATT_K = ATT_KROWS * GRID_W
N_QBLOCKS = GRID_ROWS // ATT_QROWS
MIX_TM = 256
DFT_RADIX = 4
DFT_SUB = SEQ // DFT_RADIX
DFT_TK = 256
DFT_TN = 1024
EXPERT_TM = 256
N_ROWS = N_TOK * TOP_K
N_EXPERT_BLOCKS = N_ROWS // EXPERT_TM + N_EXPERTS
N_PAD = N_EXPERT_BLOCKS * EXPERT_TM
DISPATCH_TM = 1024
WEIGHT_DMA_CHUNKS = 4
N_FILL_SEGMENTS = N_EXPERTS + 1
ROW_CHUNKS = D_MODEL // 2 // LANES
OUT_CHUNKS = D_MODEL // LANES
MIX_BLOCK = 512
MIX_SUBTILES = MIX_BLOCK // MIX_TM
MODS_TN = 1024

F32 = jnp.float32
BF16 = jnp.bfloat16


def _cparams(*sem):
    return pltpu.CompilerParams(dimension_semantics=sem, vmem_limit_bytes=V7X_VMEM_LIMIT_BYTES)


def _ln(x):
    mu = jnp.mean(x, axis=-1, keepdims=True)
    xc = x - mu
    var = jnp.mean(xc * xc, axis=-1, keepdims=True)
    return xc * lax.rsqrt(var + LN_EPS)


def _sigmoid(x):
    return 1.0 / (1.0 + jnp.exp(-x))


def _mods_kernel(c_ref, w_ref, b_ref, o_ref):
    c = c_ref[...]
    s = (c * _sigmoid(c)).astype(BF16)
    o_ref[0] = jnp.dot(s, w_ref[0].astype(BF16), preferred_element_type=F32) + b_ref[0]


def _mods(c8, ada_w, ada_b):
    n = 6 * D_MODEL
    return pl.pallas_call(
        _mods_kernel,
        grid=(DEPTH, n // MODS_TN),
        in_specs=[
            pl.BlockSpec((SUBLANES, D_MODEL), lambda l, j: (0, 0)),
            pl.BlockSpec((1, D_MODEL, MODS_TN), lambda l, j: (l, 0, j)),
            pl.BlockSpec((1, 1, MODS_TN), lambda l, j: (l, 0, j)),
        ],
        out_specs=pl.BlockSpec((1, SUBLANES, MODS_TN), lambda l, j: (l, 0, j)),
        out_shape=jax.ShapeDtypeStruct((DEPTH, SUBLANES, n), F32),
        compiler_params=_cparams("arbitrary", "arbitrary"),
        name="mods",
    )(c8, ada_w, ada_b.reshape(DEPTH, 1, n))


def _ln_proj_kernel(x_ref, m_ref, w_ref, o_ref, u_scr):
    @pl.when(pl.program_id(1) == 0)
    def _():
        m = m_ref[0]
        y = _ln(x_ref[...])
        u_scr[...] = (y * (1.0 + m[:, D_MODEL:2 * D_MODEL]) + m[:, :D_MODEL]).astype(BF16)

    o_ref[...] = jnp.dot(u_scr[...], w_ref[...], preferred_element_type=F32).astype(o_ref.dtype)


def _ln_proj(x2d, mods3, w, col_block0, n_cols, mod_row_of_tile):
    n_rows = x2d.shape[0]
    return pl.pallas_call(
        _ln_proj_kernel,
        grid=(n_rows // PROJ_TM, n_cols // PROJ_TN),
        in_specs=[
            pl.BlockSpec((PROJ_TM, D_MODEL), lambda i, j: (i, 0)),
            pl.BlockSpec((1, 1, 6 * D_MODEL), lambda i, j: (mod_row_of_tile(i), 0, 0)),
            pl.BlockSpec((D_MODEL, PROJ_TN), lambda i, j: (0, col_block0 + j)),
        ],
        out_specs=pl.BlockSpec((PROJ_TM, PROJ_TN), lambda i, j: (i, j)),
        out_shape=jax.ShapeDtypeStruct((n_rows, n_cols), BF16),
        scratch_shapes=[pltpu.VMEM((PROJ_TM, D_MODEL), BF16)],
        compiler_params=_cparams("arbitrary", "arbitrary"),
        name="ln_proj",
    )(x2d, mods3, w)


def _bias_table_kernel(rpb_ref, o_ref):
    h = pl.program_id(0)
    qc = lax.broadcasted_iota(jnp.int32, (GRID_W, GRID_W), 0)
    kc = lax.broadcasted_iota(jnp.int32, (GRID_W, GRID_W), 1)
    col_start = jnp.clip(qc - WIN_W // 2, 0, GRID_W - WIN_W)
    col_ok = (kc >= col_start) & (kc < col_start + WIN_W)
    col_idx = jnp.clip(kc - qc, -(WIN_W - 1), WIN_W - 1) + WIN_W - 1
    n_rb = 2 * WIN_H - 1
    n_cb = 2 * WIN_W - 1
    tiles = []
    for ri in range(n_rb):
        acc = jnp.zeros((GRID_W, GRID_W), F32)
        for i in range(n_cb):
            acc = jnp.where(col_idx == i, rpb_ref[(h * n_rb + ri) * n_cb + i], acc)
        tiles.append(jnp.where(col_ok, acc, NEG_INF))
    masked = jnp.full((GRID_W, GRID_W), NEG_INF, F32)
    for typ, blk in enumerate((0, 1, N_QBLOCKS - 1)):
        ks = min(max(blk * ATT_QROWS - WIN_H // 2, 0), GRID_ROWS - ATT_KROWS)
        for a in range(ATT_QROWS):
            r = blk * ATT_QROWS + a
            rs = min(max(r - WIN_H // 2, 0), GRID_ROWS - WIN_H)
            strip = [tiles[ks + c - r + WIN_H - 1] if rs <= ks + c < rs + WIN_H else masked
                     for c in range(ATT_KROWS)]
            o_ref[0, typ, a * GRID_W:(a + 1) * GRID_W, :] = jnp.concatenate(strip, axis=1)


def _bias_table(rpb_l):
    flat = rpb_l.reshape(-1).astype(F32)
    return pl.pallas_call(
        _bias_table_kernel,
        grid=(N_HEADS,),
        in_specs=[pl.BlockSpec(memory_space=pltpu.SMEM)],
        out_specs=pl.BlockSpec((1, 3, ATT_Q, ATT_K), lambda h: (h, 0, 0, 0)),
        out_shape=jax.ShapeDtypeStruct((N_HEADS, 3, ATT_Q, ATT_K), F32),
        compiler_params=_cparams("arbitrary"),
        name="bias_table",
    )(flat)


def _natten_kernel(q_ref, k_ref, v_ref, kc_ref, vc_ref, bias_ref, o_ref):
    scale = HEAD_DIM ** -0.5
    kc = kc_ref[...]
    vc = vc_ref[...]
    nt = (((1,), (1,)), ((), ()))

    def body(j, carry):
        typ = jnp.where(j == 0, 0, jnp.where(j == N_QBLOCKS - 1, 2, 1))
        q0 = pl.multiple_of(j * ATT_Q, ATT_Q)
        k0 = pl.multiple_of(
            jnp.clip(j * ATT_QROWS - WIN_H // 2, 0, GRID_ROWS - ATT_KROWS) * GRID_W, 4 * GRID_W)
        q = q_ref[pl.ds(q0, ATT_Q), :]
        kw = k_ref[pl.ds(k0, ATT_K), :]
        vw = v_ref[pl.ds(k0, ATT_K), :]
        s_loc = lax.dot_general(q, kw, nt, preferred_element_type=F32) * scale + bias_ref[typ]
        s_ctx = lax.dot_general(q, kc, nt, preferred_element_type=F32) * scale
        m = jnp.maximum(jnp.max(s_loc, axis=-1, keepdims=True), jnp.max(s_ctx, axis=-1, keepdims=True))
        p_loc = jnp.exp(s_loc - m)
        p_ctx = jnp.exp(s_ctx - m)
        denom = jnp.sum(p_loc, axis=-1, keepdims=True) + jnp.sum(p_ctx, axis=-1, keepdims=True)
        o = (jnp.dot(p_loc.astype(BF16), vw, preferred_element_type=F32)
             + jnp.dot(p_ctx.astype(BF16), vc, preferred_element_type=F32))
        o_ref[pl.ds(q0, ATT_Q), :] = (o / denom).astype(o_ref.dtype)
        return carry

    lax.fori_loop(0, N_QBLOCKS, body, 0)


def _natten(proj, kvc, bias):
    blk = lambda off: pl.BlockSpec((SEQ, HEAD_DIM), lambda b, h: (b, off + h))
    cblk = lambda off: pl.BlockSpec((CTX_LEN, HEAD_DIM), lambda b, h: (b, off + h))
    return pl.pallas_call(
        _natten_kernel,
        grid=(BATCH, N_HEADS),
        in_specs=[blk(0), blk(N_HEADS), blk(2 * N_HEADS), cblk(0), cblk(N_HEADS),
                  pl.BlockSpec((None, 3, ATT_Q, ATT_K), lambda b, h: (h, 0, 0, 0))],
        out_specs=pl.BlockSpec((SEQ, HEAD_DIM), lambda b, h: (b, h)),
        out_shape=jax.ShapeDtypeStruct((N_TOK, ATTN_WIDTH), BF16),
        compiler_params=_cparams("arbitrary", "arbitrary"),
        name="natten",
    )(proj, proj, proj, kvc, kvc, bias)


def _pack_bf16_pair(lo, hi):
    return pltpu.pack_elementwise([lo, hi], packed_dtype=BF16)


def _unpack_bf16_pair(w):
    unpack = lambda i: pltpu.unpack_elementwise(w, index=i, packed_dtype=BF16, unpacked_dtype=F32)
    return unpack(0), unpack(1)


def _store_token_tiles(ref, vals):
    tm, n = vals.shape[0], vals.shape[1] // LANES
    for c in range(n):
        ref[pl.ds(c, tm, stride=n), :] = vals[:, c * LANES:(c + 1) * LANES]


def _load_token_tiles(ref, tm):
    n = ref.shape[0] // tm
    return jnp.concatenate([ref[pl.ds(c, tm, stride=n), :] for c in range(n)], axis=1)


def _epilogue(y, s, x_ref, m_ref, lng_ref, lnb_ref, rwh_ref, rwl_ref, rb_ref,
              x1_ref, u2_ref, re_ref, rw_ref, rk_ref, cnt_ref):
    tm = MIX_TM
    rows = slice(s * tm, (s + 1) * tm)
    m = m_ref[0]
    g1 = m[:, 2 * D_MODEL:3 * D_MODEL]
    sh2 = m[:, 3 * D_MODEL:4 * D_MODEL]
    sc2 = m[:, 4 * D_MODEL:5 * D_MODEL]
    x1 = _ln(ALPHA * x_ref[rows, :] + g1 * y) * lng_ref[0:1, :] + lnb_ref[0:1, :]
    x1_ref[rows, :] = x1
    u2 = _ln(x1) * (1.0 + sc2) + sh2
    _store_token_tiles(u2_ref.at[pl.ds(s * tm * ROW_CHUNKS, tm * ROW_CHUNKS)],
                       _pack_bf16_pair(u2[:, :D_MODEL // 2], u2[:, D_MODEL // 2:]))

    u_hi = u2.astype(BF16)
    u_lo = (u2 - u_hi.astype(F32)).astype(BF16)
    logits = (jnp.dot(u_hi, rwh_ref[...], preferred_element_type=F32)
              + jnp.dot(u_lo, rwh_ref[...], preferred_element_type=F32)
              + jnp.dot(u_hi, rwl_ref[...], preferred_element_type=F32))
    aff = _sigmoid(logits.T[:N_EXPERTS, :])
    biased = aff + rb_ref[...]
    a = [biased[l * N_GROUPS:(l + 1) * N_GROUPS, :] for l in range(EXPERTS_PER_GROUP)]
    f = [aff[l * N_GROUPS:(l + 1) * N_GROUPS, :] for l in range(EXPERTS_PER_GROUP)]
    hi01, lo01 = jnp.maximum(a[0], a[1]), jnp.minimum(a[0], a[1])
    hi23, lo23 = jnp.maximum(a[2], a[3]), jnp.minimum(a[2], a[3])
    top1 = jnp.maximum(hi01, hi23)
    top2 = jnp.maximum(jnp.minimum(hi01, hi23), jnp.maximum(lo01, lo23))
    gscore = top1 + top2
    gi = lax.broadcasted_iota(jnp.int32, (N_GROUPS, tm), 0)
    gmax = jnp.max(gscore, axis=0, keepdims=True)
    group = jnp.min(jnp.where(gscore == gmax, gi, N_GROUPS), axis=0, keepdims=True)
    sel = gi == group
    b = [jnp.sum(jnp.where(sel, a[l], 0.0), axis=0, keepdims=True) for l in range(EXPERTS_PER_GROUP)]
    c = [jnp.sum(jnp.where(sel, f[l], 0.0), axis=0, keepdims=True) for l in range(EXPERTS_PER_GROUP)]

    def first_argmax(vals):
        mx = functools.reduce(jnp.maximum, vals)
        idx = jnp.full(mx.shape, len(vals) - 1, jnp.int32)
        for l in range(len(vals) - 2, -1, -1):
            idx = jnp.where(vals[l] == mx, l, idx)
        return idx

    i1 = first_argmax(b)
    i2 = first_argmax([jnp.where(i1 == l, -jnp.inf, b[l]) for l in range(EXPERTS_PER_GROUP)])
    pick = lambda idx: functools.reduce(
        lambda acc, l: jnp.where(idx == l, c[l], acc), range(1, EXPERTS_PER_GROUP), c[0])
    w1, w2 = pick(i1), pick(i2)
    wsum = w1 + w2
    e1 = group * EXPERTS_PER_GROUP + i1
    e2 = group * EXPERTS_PER_GROUP + i2
    re_ref[0:1, rows] = e1
    re_ref[1:2, rows] = e2
    rw_ref[0:1, rows] = w1 / wsum
    rw_ref[1:2, rows] = w2 / wsum

    ei = lax.broadcasted_iota(jnp.int32, (N_EXPERTS, tm), 0)
    oh1 = ei == e1
    oh2 = ei == e2
    oh = jnp.where(oh1 | oh2, 1.0, 0.0)
    t_row = lax.broadcasted_iota(jnp.int32, (tm, tm), 0)
    t_col = lax.broadcasted_iota(jnp.int32, (tm, tm), 1)
    before = jnp.where(t_row < t_col, 1.0, 0.0).astype(BF16)
    prefix = jnp.dot(oh.astype(BF16), before, preferred_element_type=F32)
    rk_ref[0:1, rows] = jnp.sum(jnp.where(oh1, prefix, 0.0), axis=0, keepdims=True).astype(jnp.int32)
    rk_ref[1:2, rows] = jnp.sum(jnp.where(oh2, prefix, 0.0), axis=0, keepdims=True).astype(jnp.int32)
    cnt = jnp.sum(oh, axis=1, keepdims=True)
    cnt_ref[s] = jnp.broadcast_to(cnt, (N_EXPERTS, LANES)).astype(jnp.int32)


def _resident(shape):
    return pl.BlockSpec(shape, lambda i: (0,) * len(shape), pipeline_mode=pl.Buffered(1))


def _epilogue_specs():
    blocks_per_seq = SEQ // MIX_BLOCK
    n_tiles = N_TOK // MIX_TM
    in_specs = [
        pl.BlockSpec((MIX_BLOCK, D_MODEL), lambda i: (i, 0)),
        pl.BlockSpec((1, 1, 6 * D_MODEL), lambda i: (i // blocks_per_seq, 0, 0)),
        _resident((2, D_MODEL)),
        _resident((2, D_MODEL)),
        _resident((D_MODEL, LANES)),
        _resident((D_MODEL, LANES)),
        _resident((N_EXPERTS, 1)),
    ]
    out_specs = [
        pl.BlockSpec((MIX_BLOCK, D_MODEL), lambda i: (i, 0)),
        pl.BlockSpec((MIX_BLOCK * ROW_CHUNKS, LANES), lambda i: (i, 0)),
        pl.BlockSpec((TOP_K, MIX_BLOCK), lambda i: (0, i)),
        pl.BlockSpec((TOP_K, MIX_BLOCK), lambda i: (0, i)),
        pl.BlockSpec((TOP_K, MIX_BLOCK), lambda i: (0, i)),
        pl.BlockSpec((MIX_SUBTILES, N_EXPERTS, LANES), lambda i: (i, 0, 0)),
    ]
    out_shape = [
        jax.ShapeDtypeStruct((N_TOK, D_MODEL), F32),
        jax.ShapeDtypeStruct((N_TOK * ROW_CHUNKS, LANES), jnp.uint32),
        jax.ShapeDtypeStruct((TOP_K, N_TOK), jnp.int32),
        jax.ShapeDtypeStruct((TOP_K, N_TOK), F32),
        jax.ShapeDtypeStruct((TOP_K, N_TOK), jnp.int32),
        jax.ShapeDtypeStruct((n_tiles, N_EXPERTS, LANES), jnp.int32),
    ]
    return in_specs, out_specs, out_shape


def _mix_out_kernel(attn_ref, gb_ref, gc_ref, xin_ref, gcp_ref, xinp_ref, gcn_ref, xinn_ref,
                    cw_ref, wo_ref, *epilogue_refs):
    i = pl.program_id(0)
    blocks_per_seq = SEQ // MIX_BLOCK
    z = gc_ref[...].astype(F32) * xin_ref[...].astype(F32)
    halo = 2 * SUBLANES
    zp_row = (gcp_ref[...].astype(F32) * xinp_ref[...].astype(F32))[halo - 1:halo, :]
    zn_row = (gcn_ref[...].astype(F32) * xinn_ref[...].astype(F32))[0:1, :]
    zp_row = jnp.where(i % blocks_per_seq == 0, 0.0, zp_row)
    zn_row = jnp.where(i % blocks_per_seq == blocks_per_seq - 1, 0.0, zn_row)
    row = lax.broadcasted_iota(jnp.int32, (MIX_BLOCK, 1), 0)
    z_prev = jnp.where(row == 0, zp_row, pltpu.roll(z, 1, axis=0))
    z_next = jnp.where(row == MIX_BLOCK - 1, zn_row, pltpu.roll(z, MIX_BLOCK - 1, axis=0))
    conv = cw_ref[0:1, :] * z_prev + cw_ref[1:2, :] * z + cw_ref[2:3, :] * z_next
    gated = (gb_ref[...].astype(F32) * conv).astype(BF16)
    y = (jnp.dot(attn_ref[...], wo_ref[:ATTN_WIDTH, :], preferred_element_type=F32)
         + jnp.dot(gated, wo_ref[ATTN_WIDTH:, :], preferred_element_type=F32))
    for s in range(MIX_SUBTILES):
        _epilogue(y[s * MIX_TM:(s + 1) * MIX_TM, :], s, *epilogue_refs)


def _mix_out(attn, proj, conv_w, w_out, x2d, mods3, ln_g, ln_b, rw_hi, rw_lo, rb):
    halo = 2 * SUBLANES
    hb = MIX_BLOCK // halo
    n_halo_blocks = N_TOK // halo
    cblk = lambda off: pl.BlockSpec((MIX_BLOCK, CONV_CH), lambda i: (i, off))
    prev = lambda off: pl.BlockSpec((halo, CONV_CH), lambda i: (jnp.maximum(i * hb - 1, 0), off))
    nxt = lambda off: pl.BlockSpec(
        (halo, CONV_CH), lambda i: (jnp.minimum((i + 1) * hb, n_halo_blocks - 1), off))
    ep_in, out_specs, out_shape = _epilogue_specs()
    return pl.pallas_call(
        _mix_out_kernel,
        grid=(N_TOK // MIX_BLOCK,),
        in_specs=[pl.BlockSpec((MIX_BLOCK, ATTN_WIDTH), lambda i: (i, 0)),
                  cblk(3), cblk(4), cblk(5), prev(4), prev(5), nxt(4), nxt(5),
                  _resident((3, CONV_CH)), _resident((D_MODEL, D_MODEL))] + ep_in,
        out_specs=out_specs,
        out_shape=out_shape,
        compiler_params=_cparams("arbitrary"),
        name="mix_out",
    )(attn, proj, proj, proj, proj, proj, proj, proj, conv_w, w_out, x2d, mods3, ln_g, ln_b,
      rw_hi, rw_lo, rb)


def _dft_seq_kernel(cs_ref, tw_ref, u0_ref, u1_ref, u2_ref, u3_ref, o_ref):
    c_m, s_m = cs_ref[0], cs_ref[1]
    tr, ti = [], []
    for n2, u_ref in enumerate((u0_ref, u1_ref, u2_ref, u3_ref)):
        x = u_ref[...]
        a = jnp.dot(c_m, x, preferred_element_type=F32)
        b = jnp.dot(s_m, x, preferred_element_type=F32)
        if n2 == 0:
            tr.append(a)
            ti.append(b)
        else:
            c = tw_ref[:, 2 * n2 - 2:2 * n2 - 1]
            s = tw_ref[:, 2 * n2 - 1:2 * n2]
            tr.append(a * c - b * s)
            ti.append(a * s + b * c)
    o_ref[0, 0] = (tr[0] + tr[1] + tr[2] + tr[3]).astype(o_ref.dtype)
    o_ref[1, 0] = (ti[0] + ti[1] + ti[2] + ti[3]).astype(o_ref.dtype)
    o_ref[0, 1] = (tr[0] - ti[1] - tr[2] + ti[3]).astype(o_ref.dtype)
    o_ref[1, 1] = (ti[0] + tr[1] - ti[2] - tr[3]).astype(o_ref.dtype)
    o_ref[0, 2] = (tr[0] - tr[1] + tr[2] - tr[3]).astype(o_ref.dtype)
    o_ref[1, 2] = (ti[0] - ti[1] + ti[2] - ti[3]).astype(o_ref.dtype)
    o_ref[0, 3] = (tr[0] + ti[1] - tr[2] - ti[3]).astype(o_ref.dtype)
    o_ref[1, 3] = (ti[0] - tr[1] - ti[2] + tr[3]).astype(o_ref.dtype)


def _dft_seq(cs, tw, u):
    u4 = u.reshape(BATCH * DFT_SUB, DFT_RADIX * D_MODEL)
    n_col = D_MODEL // DFT_TN
    sub = lambda n2: pl.BlockSpec((DFT_SUB, DFT_TN), lambda b, n, i: (b, n2 * n_col + n))
    out = pl.pallas_call(
        _dft_seq_kernel,
        grid=(BATCH, n_col, DFT_SUB // DFT_TK),
        in_specs=[pl.BlockSpec((2, DFT_TK, DFT_SUB), lambda b, n, i: (0, i, 0)),
                  pl.BlockSpec((DFT_TK, 2 * DFT_RADIX), lambda b, n, i: (i, 0)),
                  sub(0), sub(1), sub(2), sub(3)],
        out_specs=pl.BlockSpec((None, 2, DFT_RADIX, DFT_TK, DFT_TN), lambda b, n, i: (b, 0, 0, i, n)),
        out_shape=jax.ShapeDtypeStruct((BATCH, 2, DFT_RADIX, DFT_SUB, D_MODEL), BF16),
        compiler_params=_cparams("arbitrary", "arbitrary", "arbitrary"),
        name="dft_seq",
    )(cs, tw, u4, u4, u4, u4)
    return out.reshape(BATCH * 2 * SEQ, D_MODEL)


def _four_out_kernel(p_ref, q_ref, cc_ref, sc_ref, wf_ref, *epilogue_refs):
    y = jnp.zeros((MIX_BLOCK, D_MODEL), F32)
    for g in range(N_FOURIER_GROUPS):
        cols = slice(g * FOURIER_CH, (g + 1) * FOURIER_CH)
        fg = (jnp.dot(p_ref[:, cols], cc_ref[...], preferred_element_type=F32)
              - jnp.dot(q_ref[:, cols], sc_ref[...], preferred_element_type=F32))
        y = y + jnp.dot(fg.astype(BF16), wf_ref[cols, :], preferred_element_type=F32)
    for s in range(MIX_SUBTILES):
        _epilogue(y[s * MIX_TM:(s + 1) * MIX_TM, :], s, *epilogue_refs)


def _four_out(pq, cc, sc, wf, x2d, mods3, ln_g, ln_b, rw_hi, rw_lo, rb):
    blocks_per_seq = SEQ // MIX_BLOCK
    ep_in, out_specs, out_shape = _epilogue_specs()
    prow = lambda i: (i // blocks_per_seq) * 2 * blocks_per_seq + i % blocks_per_seq
    return pl.pallas_call(
        _four_out_kernel,
        grid=(N_TOK // MIX_BLOCK,),
        in_specs=[pl.BlockSpec((MIX_BLOCK, D_MODEL), lambda i: (prow(i), 0)),
                  pl.BlockSpec((MIX_BLOCK, D_MODEL), lambda i: (prow(i) + blocks_per_seq, 0)),
                  _resident((FOURIER_CH, FOURIER_CH)), _resident((FOURIER_CH, FOURIER_CH)),
                  _resident((D_MODEL, D_MODEL))] + ep_in,
        out_specs=out_specs,
        out_shape=out_shape,
        compiler_params=_cparams("arbitrary"),
        name="four_out",
    )(pq, pq, cc, sc, wf, x2d, mods3, ln_g, ln_b, rw_hi, rw_lo, rb)


def _dispatch_kernel(dest, fill_start, fill_n, u_ref, out_hbm, sem):
    i = pl.program_id(0)
    base = i * DISPATCH_TM

    def row_dma(r, d):
        src = u_ref.at[pl.ds(pl.multiple_of(r * ROW_CHUNKS, ROW_CHUNKS), ROW_CHUNKS)]
        dst = out_hbm.at[pl.ds(pl.multiple_of(d * ROW_CHUNKS, ROW_CHUNKS), ROW_CHUNKS)]
        return pltpu.make_async_copy(src, dst, sem)

    def issue(r, c):
        for k in range(TOP_K):
            row_dma(r, dest[k * N_TOK + base + r]).start(priority=k)
        return c

    def drain(r, c):
        for k in range(TOP_K):
            row_dma(0, 0).wait()
        return c

    lax.fori_loop(0, DISPATCH_TM, issue, 0, unroll=8)
    lax.fori_loop(0, DISPATCH_TM, drain, 0, unroll=8)

    @pl.when(i == pl.num_programs(0) - 1)
    def _():
        def per_segment(e, c):
            lax.fori_loop(0, fill_n[e], lambda r, c2: (row_dma(0, fill_start[e] + r).start(), c2)[1], 0)
            lax.fori_loop(0, fill_n[e], lambda r, c2: (row_dma(0, 0).wait(), c2)[1], 0)
            return c

        lax.fori_loop(0, N_FILL_SEGMENTS, per_segment, 0)


def _dispatch(u2p, dest, fill_start, fill_n):
    return pl.pallas_call(
        _dispatch_kernel,
        grid_spec=pltpu.PrefetchScalarGridSpec(
            num_scalar_prefetch=3,
            grid=(N_TOK // DISPATCH_TM,),
            in_specs=[pl.BlockSpec((DISPATCH_TM * ROW_CHUNKS, LANES), lambda i, *_: (i, 0))],
            out_specs=pl.BlockSpec(memory_space=pl.ANY),
            scratch_shapes=[pltpu.SemaphoreType.DMA(())],
        ),
        out_shape=jax.ShapeDtypeStruct((N_PAD * ROW_CHUNKS, LANES), jnp.uint32),
        compiler_params=_cparams("arbitrary"),
        name="dispatch",
    )(dest, fill_start, fill_n, u2p)


def _expert_weights(layer, be_ref, run_ref, nxt_ref, w_hbm, stage, sem, w_bf16):
    i = pl.program_id(0)
    changed = (i == 0) | (be_ref[i] != be_ref[jnp.maximum(i - 1, 0)])

    def copies(e, slot):
        out = []
        for w, st in zip(w_hbm, stage):
            rows = st.shape[1] // WEIGHT_DMA_CHUNKS
            for c in range(WEIGHT_DMA_CHUNKS):
                rs = pl.ds(c * rows, rows)
                out.append(pltpu.make_async_copy(w.at[layer, e, rs], st.at[slot, rs], sem.at[slot]))
        return out

    @pl.when(changed)
    def _():
        slot = run_ref[i] % 2

        @pl.when(i == 0)
        def _():
            for cp in copies(be_ref[0], 0):
                cp.start(priority=1)

        for cp in copies(be_ref[i], slot):
            cp.wait()

        @pl.when(nxt_ref[i] >= 0)
        def _():
            for cp in copies(nxt_ref[i], 1 - slot):
                cp.start(priority=1)

        for st, wb in zip(stage, w_bf16):
            wb[...] = st[slot].astype(BF16)


def _experts_up_kernel(be_ref, nv_ref, run_ref, nxt_ref, x_ref, wg_hbm, wu_hbm, h_ref,
                       wg_st, wu_st, sem, wg_b, wu_b, *, layer):
    _expert_weights(layer, be_ref, run_ref, nxt_ref, (wg_hbm, wu_hbm), (wg_st, wu_st), sem, (wg_b, wu_b))

    @pl.when(pl.program_id(0) >= nv_ref[0])
    def _():
        h_ref[...] = jnp.zeros_like(h_ref)

    @pl.when(pl.program_id(0) < nv_ref[0])
    def _():
        lo, hi = _unpack_bf16_pair(_load_token_tiles(x_ref, EXPERT_TM))
        lo, hi = lo.astype(BF16), hi.astype(BF16)
        half = D_MODEL // 2
        gate = (jnp.dot(lo, wg_b[:half, :], preferred_element_type=F32)
                + jnp.dot(hi, wg_b[half:, :], preferred_element_type=F32))
        up = (jnp.dot(lo, wu_b[:half, :], preferred_element_type=F32)
              + jnp.dot(hi, wu_b[half:, :], preferred_element_type=F32))
        h_ref[...] = (gate * _sigmoid(gate) * up).astype(BF16)


def _experts_down_kernel(be_ref, nv_ref, run_ref, nxt_ref, h_ref, wd_hbm, o_ref, wd_st, sem, wd_b, *, layer):
    _expert_weights(layer, be_ref, run_ref, nxt_ref, (wd_hbm,), (wd_st,), sem, (wd_b,))

    @pl.when(pl.program_id(0) >= nv_ref[0])
    def _():
        o_ref[...] = jnp.zeros_like(o_ref)

    @pl.when(pl.program_id(0) < nv_ref[0])
    def _():
        _store_token_tiles(o_ref, jnp.dot(h_ref[...], wd_b[...], preferred_element_type=F32))


def _experts(layer, sched, xs, w_gate, w_up, w_down):
    rows = lambda c: pl.BlockSpec((EXPERT_TM, c), lambda i, *_: (i, 0))
    tiles = pl.BlockSpec((EXPERT_TM * ROW_CHUNKS, LANES), lambda i, *_: (i, 0))
    any_spec = pl.BlockSpec(memory_space=pl.ANY)
    up_shape, down_shape = (D_MODEL, D_EXPERT), (D_EXPERT, D_MODEL)
    h = pl.pallas_call(
        functools.partial(_experts_up_kernel, layer=layer),
        grid_spec=pltpu.PrefetchScalarGridSpec(
            num_scalar_prefetch=4,
            grid=(N_EXPERT_BLOCKS,),
            in_specs=[tiles, any_spec, any_spec],
            out_specs=rows(D_EXPERT),
            scratch_shapes=[pltpu.VMEM((2,) + up_shape, F32), pltpu.VMEM((2,) + up_shape, F32),
                            pltpu.SemaphoreType.DMA((2,)),
                            pltpu.VMEM(up_shape, BF16), pltpu.VMEM(up_shape, BF16)],
        ),
        out_shape=jax.ShapeDtypeStruct((N_PAD, D_EXPERT), BF16),
        compiler_params=_cparams("arbitrary"),
        name="experts_up",
    )(*sched, xs, w_gate, w_up)
    return pl.pallas_call(
        functools.partial(_experts_down_kernel, layer=layer),
        grid_spec=pltpu.PrefetchScalarGridSpec(
            num_scalar_prefetch=4,
            grid=(N_EXPERT_BLOCKS,),
            in_specs=[rows(D_EXPERT), any_spec],
            out_specs=pl.BlockSpec((EXPERT_TM * OUT_CHUNKS, LANES), lambda i, *_: (i, 0)),
            scratch_shapes=[pltpu.VMEM((2,) + down_shape, F32), pltpu.SemaphoreType.DMA((2,)),
                            pltpu.VMEM(down_shape, BF16)],
        ),
        out_shape=jax.ShapeDtypeStruct((N_PAD * OUT_CHUNKS, LANES), F32),
        compiler_params=_cparams("arbitrary"),
        name="experts_down",
    )(*sched, h, w_down)


def _final_kernel(dest, ys_hbm, rw_ref, x1_ref, m_ref, lng_ref, lnb_ref, *rest, emit_next):
    if emit_next:
        mn_ref, x2_ref, un_ref, buf, sem = rest
    else:
        x2_ref, buf, sem = rest
    i = pl.program_id(0)
    n = pl.num_programs(0)

    def row_dma(tile, slot, r, k):
        d = dest[k * N_TOK + tile * MIX_TM + r]
        src = ys_hbm.at[pl.ds(pl.multiple_of(d * OUT_CHUNKS, OUT_CHUNKS), OUT_CHUNKS)]
        dst = buf.at[slot, k, pl.ds(pl.multiple_of(r * OUT_CHUNKS, OUT_CHUNKS), OUT_CHUNKS)]
        return pltpu.make_async_copy(src, dst, sem.at[slot])

    def gather(tile, slot):
        def issue(r, c):
            for k in range(TOP_K):
                row_dma(tile, slot, r, k).start(priority=k)
            return c
        lax.fori_loop(0, MIX_TM, issue, 0, unroll=8)

    @pl.when(i == 0)
    def _():
        gather(0, 0)

    @pl.when(i + 1 < n)
    def _():
        gather(i + 1, (i + 1) % 2)

    slot = i % 2

    def drain(r, c):
        for k in range(TOP_K):
            row_dma(0, slot, 0, k).wait()
        return c
    lax.fori_loop(0, MIX_TM, drain, 0, unroll=8)

    expert_rows = lambda k: _load_token_tiles(buf.at[slot, k], MIX_TM)
    m = m_ref[0]
    g2 = m[:, 5 * D_MODEL:6 * D_MODEL]
    f = rw_ref[:, 0:1] * expert_rows(0) + rw_ref[:, 1:2] * expert_rows(1)
    x2 = _ln(ALPHA * x1_ref[...] + g2 * f) * lng_ref[1:2, :] + lnb_ref[1:2, :]
    x2_ref[...] = x2
    if emit_next:
        mn = mn_ref[0]
        un_ref[...] = (_ln(x2) * (1.0 + mn[:, D_MODEL:2 * D_MODEL]) + mn[:, :D_MODEL]).astype(BF16)


def _final(dest, ys, rw_t, x1, mods3, ln_g, ln_b, mods3_next=None):
    tiles_per_seq = SEQ // MIX_TM
    emit_next = mods3_next is not None
    mspec = pl.BlockSpec((1, 1, 6 * D_MODEL), lambda i, d: (i // tiles_per_seq, 0, 0))
    row = pl.BlockSpec((MIX_TM, D_MODEL), lambda i, d: (i, 0))
    in_specs = [pl.BlockSpec(memory_space=pl.ANY),
                pl.BlockSpec((MIX_TM, TOP_K), lambda i, d: (i, 0)),
                row, mspec,
                pl.BlockSpec((2, D_MODEL), lambda i, d: (0, 0)),
                pl.BlockSpec((2, D_MODEL), lambda i, d: (0, 0))]
    args = [ys, rw_t, x1, mods3, ln_g, ln_b]
    out_specs = [row]
    out_shape = [jax.ShapeDtypeStruct((N_TOK, D_MODEL), F32)]
    if emit_next:
        in_specs.append(mspec)
        args.append(mods3_next)
        out_specs.append(row)
        out_shape.append(jax.ShapeDtypeStruct((N_TOK, D_MODEL), BF16))
    return pl.pallas_call(
        functools.partial(_final_kernel, emit_next=emit_next),
        grid_spec=pltpu.PrefetchScalarGridSpec(
            num_scalar_prefetch=1,
            grid=(N_TOK // MIX_TM,),
            in_specs=in_specs,
            out_specs=out_specs,
            scratch_shapes=[pltpu.VMEM((2, TOP_K, MIX_TM * OUT_CHUNKS, LANES), F32),
                            pltpu.SemaphoreType.DMA((2,))],
        ),
        out_shape=out_shape,
        compiler_params=_cparams("arbitrary"),
        name="final",
    )(dest, *args)


def _moe(layer, u2p, route_e, route_w, rank, cnt, w_gate, w_up, w_down):
    n_tiles = N_TOK // MIX_TM
    cnt = cnt[:, :, 0]
    counts = jnp.sum(cnt, axis=0)
    padded = (counts + EXPERT_TM - 1) // EXPERT_TM * EXPERT_TM
    pad_end = jnp.cumsum(padded)
    pad_start = pad_end - padded
    base = pad_start[None, :] + jnp.cumsum(cnt, axis=0) - cnt
    e_r = route_e.reshape(TOP_K, n_tiles, MIX_TM)
    onehot = e_r[..., None] == jnp.arange(N_EXPERTS, dtype=jnp.int32)
    dest = jnp.sum(jnp.where(onehot, base[None, :, None, :], 0), axis=-1).reshape(TOP_K, N_TOK) + rank
    dest = dest.reshape(-1).astype(jnp.int32)
    n_valid = (pad_end[-1] // EXPERT_TM).astype(jnp.int32).reshape(1)
    block_row0 = jnp.arange(N_EXPERT_BLOCKS, dtype=jnp.int32) * EXPERT_TM
    present = padded > 0
    expert_ids = jnp.arange(N_EXPERTS, dtype=jnp.int32)
    last_expert = jnp.max(jnp.where(present, expert_ids, 0))
    block_expert = jnp.minimum(
        jnp.sum(pad_end[None, :] <= block_row0[:, None], axis=1), last_expert).astype(jnp.int32)
    run_of_expert = jnp.cumsum(present) - present
    block_run = jnp.sum(
        jnp.where(expert_ids[None, :] == block_expert[:, None], run_of_expert[None, :], 0), axis=1)
    later = present[None, :] & (expert_ids[None, :] > block_expert[:, None])
    block_next = jnp.min(jnp.where(later, expert_ids[None, :], N_EXPERTS), axis=1)
    block_next = jnp.where(block_next == N_EXPERTS, -1, block_next)
    sched = (block_expert, n_valid, block_run.astype(jnp.int32), block_next.astype(jnp.int32))

    fill_start = jnp.concatenate([pad_start + counts, pad_end[-1:]]).astype(jnp.int32)
    fill_n = jnp.concatenate([padded - counts, N_PAD - pad_end[-1:]]).astype(jnp.int32)
    xs = _dispatch(u2p, dest, fill_start, fill_n)
    ys = _experts(layer, sched, xs, w_gate, w_up, w_down)
    return dest, ys, route_w.T


def _dft_tables():
    dft = lambda n: 2 * np.pi * (np.outer(np.arange(n), np.arange(n)) % n) / n
    ang = dft(DFT_SUB)
    cs = jnp.asarray(np.stack([np.cos(ang), np.sin(ang)]) * SEQ ** -0.5, F32).astype(BF16)
    k1 = np.arange(DFT_SUB, dtype=np.float64)
    tw = np.zeros((DFT_SUB, 2 * DFT_RADIX), np.float64)
    for n2 in range(1, DFT_RADIX):
        tw[:, 2 * n2 - 2] = np.cos(2 * np.pi * n2 * k1 / SEQ)
        tw[:, 2 * n2 - 1] = np.sin(2 * np.pi * n2 * k1 / SEQ)
    ang = dft(FOURIER_CH)
    cc = jnp.asarray(np.cos(ang) * FOURIER_CH ** -0.5, F32).astype(BF16)
    sc = jnp.asarray(np.sin(ang) * FOURIER_CH ** -0.5, F32).astype(BF16)
    return cs, jnp.asarray(tw, F32), cc, sc


def kernel(x, c, ctx, c_ctx, ada_w, ada_b, w_mix_in, rpb, conv_w, w_mix_out, w_fourier_out,
           router_w, router_b, w_gate, w_up, w_down, ln_g, ln_b):
    x2d = x.reshape(N_TOK, D_MODEL)
    c8 = jnp.concatenate(
        [c, c_ctx[None, :], jnp.zeros((SUBLANES - BATCH - 1, D_MODEL), F32)], axis=0)
    mods = _mods(c8, ada_w, ada_b)
    mods3 = [mods[l].reshape(SUBLANES, 1, 6 * D_MODEL) for l in range(DEPTH)]
    perm = lambda a: a.reshape(N_GROUPS, EXPERTS_PER_GROUP, -1).transpose(1, 0, 2).reshape(N_EXPERTS, -1)
    rw = jnp.pad(perm(router_w.T).T, ((0, 0), (0, LANES - N_EXPERTS)))
    rw_hi = rw.astype(BF16)
    rw_lo = (rw - rw_hi.astype(F32)).astype(BF16)
    rb = perm(router_b.reshape(N_EXPERTS, 1))
    tiles_per_seq = SEQ // PROJ_TM

    w_in = w_mix_in[0].astype(BF16)
    proj = _ln_proj(x2d, mods3[0], w_in, 0, MIX_IN_WIDTH, lambda i: i // tiles_per_seq)
    kvc = _ln_proj(ctx.reshape(BATCH * CTX_LEN, D_MODEL), mods3[0], w_in,
                   ATTN_WIDTH // PROJ_TN, 2 * ATTN_WIDTH, lambda i: BATCH)
    attn = _natten(proj, kvc, _bias_table(rpb[0]))
    x1, u2p, route_e, route_w, rank, cnt = _mix_out(
        attn, proj, conv_w[0], w_mix_out[0].astype(BF16), x2d, mods3[0], ln_g[0], ln_b[0],
        rw_hi, rw_lo, rb)
    dest, ys, rw_t = _moe(0, u2p, route_e, route_w, rank, cnt, w_gate, w_up, w_down)
    x2, u_next = _final(dest, ys, rw_t, x1, mods3[0], ln_g[0], ln_b[0], mods3_next=mods3[1])

    cs, tw, cc, sc = _dft_tables()
    pq = _dft_seq(cs, tw, u_next)
    x1, u2p, route_e, route_w, rank, cnt = _four_out(
        pq, cc, sc, w_fourier_out[0].astype(BF16), x2, mods3[1], ln_g[1], ln_b[1],
        rw_hi, rw_lo, rb)
    dest, ys, rw_t = _moe(1, u2p, route_e, route_w, rank, cnt, w_gate, w_up, w_down)
    (x3,) = _final(dest, ys, rw_t, x1, mods3[1], ln_g[1], ln_b[1])
    return x3.reshape(BATCH, SEQ, D_MODEL)
```

```python
import functools

import numpy as np
import jax
import jax.numpy as jnp
from jax import lax
from jax.experimental import pallas as pl
from jax.experimental.pallas import tpu as pltpu

D_MODEL = 2048
BATCH = 4
SEQ = 4096
DEPTH = 2
GRID_W = 64
GRID_ROWS = SEQ // GRID_W
CTX_LEN = 256
HEAD_DIM = 128
ATTN_WIDTH = D_MODEL // 2
N_HEADS = ATTN_WIDTH // HEAD_DIM
WIN_H = 8
WIN_W = 16
CONV_CH = D_MODEL // 2
MIX_IN_WIDTH = 3 * ATTN_WIDTH + 3 * CONV_CH
N_FOURIER_GROUPS = 4
FOURIER_CH = D_MODEL // N_FOURIER_GROUPS
N_EXPERTS = 32
N_GROUPS = 8
EXPERTS_PER_GROUP = N_EXPERTS // N_GROUPS
TOP_K = 2
D_EXPERT = D_MODEL // 2
ALPHA = (2 * DEPTH) ** 0.25
LN_EPS = 1e-6
NEG_INF = -1e30
N_TOK = BATCH * SEQ

V7X_VMEM_LIMIT_BYTES = 56 * 1024 * 1024
SUBLANES = 8
LANES = 128

PROJ_TM = 512
PROJ_TN = 1024
ATT_QROWS = 4
ATT_KROWS = 12
ATT_Q = ATT_QROWS * GRID_W
ATT_K = ATT_KROWS * GRID_W
N_QBLOCKS = GRID_ROWS // ATT_QROWS
MIX_TM = 256
DFT_RADIX = 4
DFT_SUB = SEQ // DFT_RADIX
DFT_TK = 256
DFT_TN = 1024
EXPERT_TM = 256
N_ROWS = N_TOK * TOP_K
N_EXPERT_BLOCKS = N_ROWS // EXPERT_TM + N_EXPERTS
N_PAD = N_EXPERT_BLOCKS * EXPERT_TM
DISPATCH_TM = 1024
WEIGHT_DMA_CHUNKS = 4
N_FILL_SEGMENTS = N_EXPERTS + 1
ROW_CHUNKS = D_MODEL // 2 // LANES
OUT_CHUNKS = D_MODEL // LANES
MIX_BLOCK = 512
MIX_SUBTILES = MIX_BLOCK // MIX_TM
MODS_TN = 1024

F32 = jnp.float32
BF16 = jnp.bfloat16


def _cparams(*sem):
    return pltpu.CompilerParams(dimension_semantics=sem, vmem_limit_bytes=V7X_VMEM_LIMIT_BYTES)


def _ln(x):
    mu = jnp.mean(x, axis=-1, keepdims=True)
    xc = x - mu
    var = jnp.mean(xc * xc, axis=-1, keepdims=True)
    return xc * lax.rsqrt(var + LN_EPS)


def _sigmoid(x):
    return 1.0 / (1.0 + jnp.exp(-x))


def _mods_kernel(c_ref, w_ref, b_ref, o_ref):
    c = c_ref[...]
    s = (c * _sigmoid(c)).astype(BF16)
    o_ref[0] = jnp.dot(s, w_ref[0].astype(BF16), preferred_element_type=F32) + b_ref[0]


def _mods(c8, ada_w, ada_b):
    n = 6 * D_MODEL
    return pl.pallas_call(
        _mods_kernel,
        grid=(DEPTH, n // MODS_TN),
        in_specs=[
            pl.BlockSpec((SUBLANES, D_MODEL), lambda l, j: (0, 0)),
            pl.BlockSpec((1, D_MODEL, MODS_TN), lambda l, j: (l, 0, j)),
            pl.BlockSpec((1, 1, MODS_TN), lambda l, j: (l, 0, j)),
        ],
        out_specs=pl.BlockSpec((1, SUBLANES, MODS_TN), lambda l, j: (l, 0, j)),
        out_shape=jax.ShapeDtypeStruct((DEPTH, SUBLANES, n), F32),
        compiler_params=_cparams("arbitrary", "arbitrary"),
        name="mods",
    )(c8, ada_w, ada_b.reshape(DEPTH, 1, n))


def _ln_proj_kernel(x_ref, m_ref, w_ref, o_ref, u_scr):
    @pl.when(pl.program_id(1) == 0)
    def _():
        m = m_ref[0]
        y = _ln(x_ref[...])
        u_scr[...] = (y * (1.0 + m[:, D_MODEL:2 * D_MODEL]) + m[:, :D_MODEL]).astype(BF16)

    o_ref[...] = jnp.dot(u_scr[...], w_ref[...], preferred_element_type=F32).astype(o_ref.dtype)


def _ln_proj(x2d, mods3, w, col_block0, n_cols, mod_row_of_tile):
    n_rows = x2d.shape[0]
    return pl.pallas_call(
        _ln_proj_kernel,
        grid=(n_rows // PROJ_TM, n_cols // PROJ_TN),
        in_specs=[
            pl.BlockSpec((PROJ_TM, D_MODEL), lambda i, j: (i, 0)),
            pl.BlockSpec((1, 1, 6 * D_MODEL), lambda i, j: (mod_row_of_tile(i), 0, 0)),
            pl.BlockSpec((D_MODEL, PROJ_TN), lambda i, j: (0, col_block0 + j)),
        ],
        out_specs=pl.BlockSpec((PROJ_TM, PROJ_TN), lambda i, j: (i, j)),
        out_shape=jax.ShapeDtypeStruct((n_rows, n_cols), BF16),
        scratch_shapes=[pltpu.VMEM((PROJ_TM, D_MODEL), BF16)],
        compiler_params=_cparams("arbitrary", "arbitrary"),
        name="ln_proj",
    )(x2d, mods3, w)


def _bias_table_kernel(rpb_ref, o_ref):
    h = pl.program_id(0)
    qc = lax.broadcasted_iota(jnp.int32, (GRID_W, GRID_W), 0)
    kc = lax.broadcasted_iota(jnp.int32, (GRID_W, GRID_W), 1)
    col_start = jnp.clip(qc - WIN_W // 2, 0, GRID_W - WIN_W)
    col_ok = (kc >= col_start) & (kc < col_start + WIN_W)
    col_idx = jnp.clip(kc - qc, -(WIN_W - 1), WIN_W - 1) + WIN_W - 1
    n_rb = 2 * WIN_H - 1
    n_cb = 2 * WIN_W - 1
    tiles = []
    for ri in range(n_rb):
        acc = jnp.zeros((GRID_W, GRID_W), F32)
        for i in range(n_cb):
            acc = jnp.where(col_idx == i, rpb_ref[(h * n_rb + ri) * n_cb + i], acc)
        tiles.append(jnp.where(col_ok, acc, NEG_INF))
    masked = jnp.full((GRID_W, GRID_W), NEG_INF, F32)
    for typ, blk in enumerate((0, 1, N_QBLOCKS - 1)):
        ks = min(max(blk * ATT_QROWS - WIN_H // 2, 0), GRID_ROWS - ATT_KROWS)
        for a in range(ATT_QROWS):
            r = blk * ATT_QROWS + a
            rs = min(max(r - WIN_H // 2, 0), GRID_ROWS - WIN_H)
            strip = [tiles[ks + c - r + WIN_H - 1] if rs <= ks + c < rs + WIN_H else masked
                     for c in range(ATT_KROWS)]
            o_ref[0, typ, a * GRID_W:(a + 1) * GRID_W, :] = jnp.concatenate(strip, axis=1)


def _bias_table(rpb_l):
    flat = rpb_l.reshape(-1).astype(F32)
    return pl.pallas_call(
        _bias_table_kernel,
        grid=(N_HEADS,),
        in_specs=[pl.BlockSpec(memory_space=pltpu.SMEM)],
        out_specs=pl.BlockSpec((1, 3, ATT_Q, ATT_K), lambda h: (h, 0, 0, 0)),
        out_shape=jax.ShapeDtypeStruct((N_HEADS, 3, ATT_Q, ATT_K), F32),
        compiler_params=_cparams("arbitrary"),
        name="bias_table",
    )(flat)


def _natten_kernel(q_ref, k_ref, v_ref, kc_ref, vc_ref, bias_ref, o_ref):
    scale = HEAD_DIM ** -0.5
    kc = kc_ref[...]
    vc = vc_ref[...]
    nt = (((1,), (1,)), ((), ()))

    def body(j, carry):
        typ = jnp.where(j == 0, 0, jnp.where(j == N_QBLOCKS - 1, 2, 1))
        q0 = pl.multiple_of(j * ATT_Q, ATT_Q)
        k0 = pl.multiple_of(
            jnp.clip(j * ATT_QROWS - WIN_H // 2, 0, GRID_ROWS - ATT_KROWS) * GRID_W, 4 * GRID_W)
        q = q_ref[pl.ds(q0, ATT_Q), :]
        kw = k_ref[pl.ds(k0, ATT_K), :]
        vw = v_ref[pl.ds(k0, ATT_K), :]
        s_loc = lax.dot_general(q, kw, nt, preferred_element_type=F32) * scale + bias_ref[typ]
        s_ctx = lax.dot_general(q, kc, nt, preferred_element_type=F32) * scale
        m = jnp.maximum(jnp.max(s_loc, axis=-1, keepdims=True), jnp.max(s_ctx, axis=-1, keepdims=True))
        p_loc = jnp.exp(s_loc - m)
        p_ctx = jnp.exp(s_ctx - m)
        denom = jnp.sum(p_loc, axis=-1, keepdims=True) + jnp.sum(p_ctx, axis=-1, keepdims=True)
        o = (jnp.dot(p_loc.astype(BF16), vw, preferred_element_type=F32)
             + jnp.dot(p_ctx.astype(BF16), vc, preferred_element_type=F32))
        o_ref[pl.ds(q0, ATT_Q), :] = (o / denom).astype(o_ref.dtype)
        return carry

    lax.fori_loop(0, N_QBLOCKS, body, 0)


def _natten(proj, kvc, bias):
    blk = lambda off: pl.BlockSpec((SEQ, HEAD_DIM), lambda b, h: (b, off + h))
    cblk = lambda off: pl.BlockSpec((CTX_LEN, HEAD_DIM), lambda b, h: (b, off + h))
    return pl.pallas_call(
        _natten_kernel,
        grid=(BATCH, N_HEADS),
        in_specs=[blk(0), blk(N_HEADS), blk(2 * N_HEADS), cblk(0), cblk(N_HEADS),
                  pl.BlockSpec((None, 3, ATT_Q, ATT_K), lambda b, h: (h, 0, 0, 0))],
        out_specs=pl.BlockSpec((SEQ, HEAD_DIM), lambda b, h: (b, h)),
        out_shape=jax.ShapeDtypeStruct((N_TOK, ATTN_WIDTH), BF16),
        compiler_params=_cparams("arbitrary", "arbitrary"),
        name="natten",
    )(proj, proj, proj, kvc, kvc, bias)


def _pack_bf16_pair(lo, hi):
    return pltpu.pack_elementwise([lo, hi], packed_dtype=BF16)


def _unpack_bf16_pair(w):
    unpack = lambda i: pltpu.unpack_elementwise(w, index=i, packed_dtype=BF16, unpacked_dtype=F32)
    return unpack(0), unpack(1)


def _store_token_tiles(ref, vals):
    tm, n = vals.shape[0], vals.shape[1] // LANES
    for c in range(n):
        ref[pl.ds(c, tm, stride=n), :] = vals[:, c * LANES:(c + 1) * LANES]


def _load_token_tiles(ref, tm):
    n = ref.shape[0] // tm
    return jnp.concatenate([ref[pl.ds(c, tm, stride=n), :] for c in range(n)], axis=1)


def _epilogue(y, s, x_ref, m_ref, lng_ref, lnb_ref, rwh_ref, rwl_ref, rb_ref,
              x1_ref, u2_ref, re_ref, rw_ref, rk_ref, cnt_ref):
    tm = MIX_TM
    rows = slice(s * tm, (s + 1) * tm)
    m = m_ref[0]
    g1 = m[:, 2 * D_MODEL:3 * D_MODEL]
    sh2 = m[:, 3 * D_MODEL:4 * D_MODEL]
    sc2 = m[:, 4 * D_MODEL:5 * D_MODEL]
    x1 = _ln(ALPHA * x_ref[rows, :] + g1 * y) * lng_ref[0:1, :] + lnb_ref[0:1, :]
    x1_ref[rows, :] = x1
    u2 = _ln(x1) * (1.0 + sc2) + sh2
    _store_token_tiles(u2_ref.at[pl.ds(s * tm * ROW_CHUNKS, tm * ROW_CHUNKS)],
                       _pack_bf16_pair(u2[:, :D_MODEL // 2], u2[:, D_MODEL // 2:]))

    u_hi = u2.astype(BF16)
    u_lo = (u2 - u_hi.astype(F32)).astype(BF16)
    logits = (jnp.dot(u_hi, rwh_ref[...], preferred_element_type=F32)
              + jnp.dot(u_lo, rwh_ref[...], preferred_element_type=F32)
              + jnp.dot(u_hi, rwl_ref[...], preferred_element_type=F32))
    aff = _sigmoid(logits.T[:N_EXPERTS, :])
    biased = aff + rb_ref[...]
    a = [biased[l * N_GROUPS:(l + 1) * N_GROUPS, :] for l in range(EXPERTS_PER_GROUP)]
    f = [aff[l * N_GROUPS:(l + 1) * N_GROUPS, :] for l in range(EXPERTS_PER_GROUP)]
    hi01, lo01 = jnp.maximum(a[0], a[1]), jnp.minimum(a[0], a[1])
    hi23, lo23 = jnp.maximum(a[2], a[3]), jnp.minimum(a[2], a[3])
    top1 = jnp.maximum(hi01, hi23)
    top2 = jnp.maximum(jnp.minimum(hi01, hi23), jnp.maximum(lo01, lo23))
    gscore = top1 + top2
    gi = lax.broadcasted_iota(jnp.int32, (N_GROUPS, tm), 0)
    gmax = jnp.max(gscore, axis=0, keepdims=True)
    group = jnp.min(jnp.where(gscore == gmax, gi, N_GROUPS), axis=0, keepdims=True)
    sel = gi == group
    b = [jnp.sum(jnp.where(sel, a[l], 0.0), axis=0, keepdims=True) for l in range(EXPERTS_PER_GROUP)]
    c = [jnp.sum(jnp.where(sel, f[l], 0.0), axis=0, keepdims=True) for l in range(EXPERTS_PER_GROUP)]

    def first_argmax(vals):
        mx = functools.reduce(jnp.maximum, vals)
        idx = jnp.full(mx.shape, len(vals) - 1, jnp.int32)
        for l in range(len(vals) - 2, -1, -1):
            idx = jnp.where(vals[l] == mx, l, idx)
        return idx

    i1 = first_argmax(b)
    i2 = first_argmax([jnp.where(i1 == l, -jnp.inf, b[l]) for l in range(EXPERTS_PER_GROUP)])
    pick = lambda idx: functools.reduce(
        lambda acc, l: jnp.where(idx == l, c[l], acc), range(1, EXPERTS_PER_GROUP), c[0])
    w1, w2 = pick(i1), pick(i2)
    wsum = w1 + w2
    e1 = group * EXPERTS_PER_GROUP + i1
    e2 = group * EXPERTS_PER_GROUP + i2
    re_ref[0:1, rows] = e1
    re_ref[1:2, rows] = e2
    rw_ref[0:1, rows] = w1 / wsum
    rw_ref[1:2, rows] = w2 / wsum

    ei = lax.broadcasted_iota(jnp.int32, (N_EXPERTS, tm), 0)
    oh1 = ei == e1
    oh2 = ei == e2
    oh = jnp.where(oh1 | oh2, 1.0, 0.0)
    t_row = lax.broadcasted_iota(jnp.int32, (tm, tm), 0)
    t_col = lax.broadcasted_iota(jnp.int32, (tm, tm), 1)
    before = jnp.where(t_row < t_col, 1.0, 0.0).astype(BF16)
    prefix = jnp.dot(oh.astype(BF16), before, preferred_element_type=F32)
    rk_ref[0:1, rows] = jnp.sum(jnp.where(oh1, prefix, 0.0), axis=0, keepdims=True).astype(jnp.int32)
    rk_ref[1:2, rows] = jnp.sum(jnp.where(oh2, prefix, 0.0), axis=0, keepdims=True).astype(jnp.int32)
    cnt = jnp.sum(oh, axis=1, keepdims=True)
    cnt_ref[s] = jnp.broadcast_to(cnt, (N_EXPERTS, LANES)).astype(jnp.int32)


def _resident(shape):
    return pl.BlockSpec(shape, lambda i: (0,) * len(shape), pipeline_mode=pl.Buffered(1))


def _epilogue_specs():
    blocks_per_seq = SEQ // MIX_BLOCK
    n_tiles = N_TOK // MIX_TM
    in_specs = [
        pl.BlockSpec((MIX_BLOCK, D_MODEL), lambda i: (i, 0)),
        pl.BlockSpec((1, 1, 6 * D_MODEL), lambda i: (i // blocks_per_seq, 0, 0)),
        _resident((2, D_MODEL)),
        _resident((2, D_MODEL)),
        _resident((D_MODEL, LANES)),
        _resident((D_MODEL, LANES)),
        _resident((N_EXPERTS, 1)),
    ]
    out_specs = [
        pl.BlockSpec((MIX_BLOCK, D_MODEL), lambda i: (i, 0)),
        pl.BlockSpec((MIX_BLOCK * ROW_CHUNKS, LANES), lambda i: (i, 0)),
        pl.BlockSpec((TOP_K, MIX_BLOCK), lambda i: (0, i)),
        pl.BlockSpec((TOP_K, MIX_BLOCK), lambda i: (0, i)),
        pl.BlockSpec((TOP_K, MIX_BLOCK), lambda i: (0, i)),
        pl.BlockSpec((MIX_SUBTILES, N_EXPERTS, LANES), lambda i: (i, 0, 0)),
    ]
    out_shape = [
        jax.ShapeDtypeStruct((N_TOK, D_MODEL), F32),
        jax.ShapeDtypeStruct((N_TOK * ROW_CHUNKS, LANES), jnp.uint32),
        jax.ShapeDtypeStruct((TOP_K, N_TOK), jnp.int32),
        jax.ShapeDtypeStruct((TOP_K, N_TOK), F32),
        jax.ShapeDtypeStruct((TOP_K, N_TOK), jnp.int32),
        jax.ShapeDtypeStruct((n_tiles, N_EXPERTS, LANES), jnp.int32),
    ]
    return in_specs, out_specs, out_shape


def _mix_out_kernel(attn_ref, gb_ref, gc_ref, xin_ref, gcp_ref, xinp_ref, gcn_ref, xinn_ref,
                    cw_ref, wo_ref, *epilogue_refs):
    i = pl.program_id(0)
    blocks_per_seq = SEQ // MIX_BLOCK
    z = gc_ref[...].astype(F32) * xin_ref[...].astype(F32)
    halo = 2 * SUBLANES
    zp_row = (gcp_ref[...].astype(F32) * xinp_ref[...].astype(F32))[halo - 1:halo, :]
    zn_row = (gcn_ref[...].astype(F32) * xinn_ref[...].astype(F32))[0:1, :]
    zp_row = jnp.where(i % blocks_per_seq == 0, 0.0, zp_row)
    zn_row = jnp.where(i % blocks_per_seq == blocks_per_seq - 1, 0.0, zn_row)
    row = lax.broadcasted_iota(jnp.int32, (MIX_BLOCK, 1), 0)
    z_prev = jnp.where(row == 0, zp_row, pltpu.roll(z, 1, axis=0))
    z_next = jnp.where(row == MIX_BLOCK - 1, zn_row, pltpu.roll(z, MIX_BLOCK - 1, axis=0))
    conv = cw_ref[0:1, :] * z_prev + cw_ref[1:2, :] * z + cw_ref[2:3, :] * z_next
    gated = (gb_ref[...].astype(F32) * conv).astype(BF16)
    y = (jnp.dot(attn_ref[...], wo_ref[:ATTN_WIDTH, :], preferred_element_type=F32)
         + jnp.dot(gated, wo_ref[ATTN_WIDTH:, :], preferred_element_type=F32))
    for s in range(MIX_SUBTILES):
        _epilogue(y[s * MIX_TM:(s + 1) * MIX_TM, :], s, *epilogue_refs)


def _mix_out(attn, proj, conv_w, w_out, x2d, mods3, ln_g, ln_b, rw_hi, rw_lo, rb):
    halo = 2 * SUBLANES
    hb = MIX_BLOCK // halo
    n_halo_blocks = N_TOK // halo
    cblk = lambda off: pl.BlockSpec((MIX_BLOCK, CONV_CH), lambda i: (i, off))
    prev = lambda off: pl.BlockSpec((halo, CONV_CH), lambda i: (jnp.maximum(i * hb - 1, 0), off))
    nxt = lambda off: pl.BlockSpec(
        (halo, CONV_CH), lambda i: (jnp.minimum((i + 1) * hb, n_halo_blocks - 1), off))
    ep_in, out_specs, out_shape = _epilogue_specs()
    return pl.pallas_call(
        _mix_out_kernel,
        grid=(N_TOK // MIX_BLOCK,),
        in_specs=[pl.BlockSpec((MIX_BLOCK, ATTN_WIDTH), lambda i: (i, 0)),
                  cblk(3), cblk(4), cblk(5), prev(4), prev(5), nxt(4), nxt(5),
                  _resident((3, CONV_CH)), _resident((D_MODEL, D_MODEL))] + ep_in,
        out_specs=out_specs,
        out_shape=out_shape,
        compiler_params=_cparams("arbitrary"),
        name="mix_out",
    )(attn, proj, proj, proj, proj, proj, proj, proj, conv_w, w_out, x2d, mods3, ln_g, ln_b,
      rw_hi, rw_lo, rb)


def _dft_seq_kernel(cs_ref, tw_ref, u0_ref, u1_ref, u2_ref, u3_ref, o_ref):
    c_m, s_m = cs_ref[0], cs_ref[1]
    tr, ti = [], []
    for n2, u_ref in enumerate((u0_ref, u1_ref, u2_ref, u3_ref)):
        x = u_ref[...]
        a = jnp.dot(c_m, x, preferred_element_type=F32)
        b = jnp.dot(s_m, x, preferred_element_type=F32)
        if n2 == 0:
            tr.append(a)
            ti.append(b)
        else:
            c = tw_ref[:, 2 * n2 - 2:2 * n2 - 1]
            s = tw_ref[:, 2 * n2 - 1:2 * n2]
            tr.append(a * c - b * s)
            ti.append(a * s + b * c)
    o_ref[0, 0] = (tr[0] + tr[1] + tr[2] + tr[3]).astype(o_ref.dtype)
    o_ref[1, 0] = (ti[0] + ti[1] + ti[2] + ti[3]).astype(o_ref.dtype)
    o_ref[0, 1] = (tr[0] - ti[1] - tr[2] + ti[3]).astype(o_ref.dtype)
    o_ref[1, 1] = (ti[0] + tr[1] - ti[2] - tr[3]).astype(o_ref.dtype)
    o_ref[0, 2] = (tr[0] - tr[1] + tr[2] - tr[3]).astype(o_ref.dtype)
    o_ref[1, 2] = (ti[0] - ti[1] + ti[2] - ti[3]).astype(o_ref.dtype)
    o_ref[0, 3] = (tr[0] + ti[1] - tr[2] - ti[3]).astype(o_ref.dtype)
    o_ref[1, 3] = (ti[0] - tr[1] - ti[2] + tr[3]).astype(o_ref.dtype)


def _dft_seq(cs, tw, u4):
    n_col = D_MODEL // DFT_TN
    sub = lambda n2: pl.BlockSpec((DFT_SUB, DFT_TN), lambda b, n, i: (b, n2 * n_col + n))
    out = pl.pallas_call(
        _dft_seq_kernel,
        grid=(BATCH, n_col, DFT_SUB // DFT_TK),
        in_specs=[pl.BlockSpec((2, DFT_TK, DFT_SUB), lambda b, n, i: (0, i, 0)),
                  pl.BlockSpec((DFT_TK, 2 * DFT_RADIX), lambda b, n, i: (i, 0)),
                  sub(0), sub(1), sub(2), sub(3)],
        out_specs=pl.BlockSpec((None, 2, DFT_RADIX, DFT_TK, DFT_TN), lambda b, n, i: (b, 0, 0, i, n)),
        out_shape=jax.ShapeDtypeStruct((BATCH, 2, DFT_RADIX, DFT_SUB, D_MODEL), BF16),
        compiler_params=_cparams("arbitrary", "arbitrary", "arbitrary"),
        name="dft_seq",
    )(cs, tw, u4, u4, u4, u4)
    return out.reshape(BATCH * 2 * SEQ, D_MODEL)


def _four_out_kernel(p_ref, q_ref, cc_ref, sc_ref, wf_ref, *epilogue_refs):
    y = jnp.zeros((MIX_BLOCK, D_MODEL), F32)
    for g in range(N_FOURIER_GROUPS):
        cols = slice(g * FOURIER_CH, (g + 1) * FOURIER_CH)
        fg = (jnp.dot(p_ref[:, cols], cc_ref[...], preferred_element_type=F32)
              - jnp.dot(q_ref[:, cols], sc_ref[...], preferred_element_type=F32))
        y = y + jnp.dot(fg.astype(BF16), wf_ref[cols, :], preferred_element_type=F32)
    for s in range(MIX_SUBTILES):
        _epilogue(y[s * MIX_TM:(s + 1) * MIX_TM, :], s, *epilogue_refs)


def _four_out(pq, cc, sc, wf, x2d, mods3, ln_g, ln_b, rw_hi, rw_lo, rb):
    blocks_per_seq = SEQ // MIX_BLOCK
    ep_in, out_specs, out_shape = _epilogue_specs()
    prow = lambda i: (i // blocks_per_seq) * 2 * blocks_per_seq + i % blocks_per_seq
    return pl.pallas_call(
        _four_out_kernel,
        grid=(N_TOK // MIX_BLOCK,),
        in_specs=[pl.BlockSpec((MIX_BLOCK, D_MODEL), lambda i: (prow(i), 0)),
                  pl.BlockSpec((MIX_BLOCK, D_MODEL), lambda i: (prow(i) + blocks_per_seq, 0)),
                  _resident((FOURIER_CH, FOURIER_CH)), _resident((FOURIER_CH, FOURIER_CH)),
                  _resident((D_MODEL, D_MODEL))] + ep_in,
        out_specs=out_specs,
        out_shape=out_shape,
        compiler_params=_cparams("arbitrary"),
        name="four_out",
    )(pq, pq, cc, sc, wf, x2d, mods3, ln_g, ln_b, rw_hi, rw_lo, rb)


def _dispatch_kernel(dest, fill_start, fill_n, u_ref, out_hbm, sem):
    i = pl.program_id(0)
    base = i * DISPATCH_TM

    def row_dma(r, d):
        src = u_ref.at[pl.ds(pl.multiple_of(r * ROW_CHUNKS, ROW_CHUNKS), ROW_CHUNKS)]
        dst = out_hbm.at[pl.ds(pl.multiple_of(d * ROW_CHUNKS, ROW_CHUNKS), ROW_CHUNKS)]
        return pltpu.make_async_copy(src, dst, sem)

    def issue(r, c):
        for k in range(TOP_K):
            row_dma(r, dest[k * N_TOK + base + r]).start(priority=k)
        return c

    def drain(r, c):
        for k in range(TOP_K):
            row_dma(0, 0).wait()
        return c

    lax.fori_loop(0, DISPATCH_TM, issue, 0, unroll=8)
    lax.fori_loop(0, DISPATCH_TM, drain, 0, unroll=8)

    @pl.when(i == pl.num_programs(0) - 1)
    def _():
        def per_segment(e, c):
            lax.fori_loop(0, fill_n[e], lambda r, c2: (row_dma(0, fill_start[e] + r).start(), c2)[1], 0)
            lax.fori_loop(0, fill_n[e], lambda r, c2: (row_dma(0, 0).wait(), c2)[1], 0)
            return c

        lax.fori_loop(0, N_FILL_SEGMENTS, per_segment, 0)


def _dispatch(u2p, dest, fill_start, fill_n):
    return pl.pallas_call(
        _dispatch_kernel,
        grid_spec=pltpu.PrefetchScalarGridSpec(
            num_scalar_prefetch=3,
            grid=(N_TOK // DISPATCH_TM,),
            in_specs=[pl.BlockSpec((DISPATCH_TM * ROW_CHUNKS, LANES), lambda i, *_: (i, 0))],
            out_specs=pl.BlockSpec(memory_space=pl.ANY),
            scratch_shapes=[pltpu.SemaphoreType.DMA(())],
        ),
        out_shape=jax.ShapeDtypeStruct((N_PAD * ROW_CHUNKS, LANES), jnp.uint32),
        compiler_params=_cparams("arbitrary"),
        name="dispatch",
    )(dest, fill_start, fill_n, u2p)


def _expert_weights(layer, be_ref, run_ref, nxt_ref, w_hbm, stage, sem, w_bf16):
    i = pl.program_id(0)
    changed = (i == 0) | (be_ref[i] != be_ref[jnp.maximum(i - 1, 0)])

    def copies(e, slot):
        out = []
        for w, st in zip(w_hbm, stage):
            rows = st.shape[1] // WEIGHT_DMA_CHUNKS
            for c in range(WEIGHT_DMA_CHUNKS):
                rs = pl.ds(c * rows, rows)
                out.append(pltpu.make_async_copy(w.at[layer, e, rs], st.at[slot, rs], sem.at[slot]))
        return out

    @pl.when(changed)
    def _():
        slot = run_ref[i] % 2

        @pl.when(i == 0)
        def _():
            for cp in copies(be_ref[0], 0):
                cp.start(priority=1)

        for cp in copies(be_ref[i], slot):
            cp.wait()

        @pl.when(nxt_ref[i] >= 0)
        def _():
            for cp in copies(nxt_ref[i], 1 - slot):
                cp.start(priority=1)

        for st, wb in zip(stage, w_bf16):
            wb[...] = st[slot].astype(BF16)


def _experts_up_kernel(be_ref, nv_ref, run_ref, nxt_ref, x_ref, wg_hbm, wu_hbm, h_ref,
                       wg_st, wu_st, sem, wg_b, wu_b, *, layer):
    _expert_weights(layer, be_ref, run_ref, nxt_ref, (wg_hbm, wu_hbm), (wg_st, wu_st), sem, (wg_b, wu_b))

    @pl.when(pl.program_id(0) >= nv_ref[0])
    def _():
        h_ref[...] = jnp.zeros_like(h_ref)

    @pl.when(pl.program_id(0) < nv_ref[0])
    def _():
        lo, hi = _unpack_bf16_pair(_load_token_tiles(x_ref, EXPERT_TM))
        lo, hi = lo.astype(BF16), hi.astype(BF16)
        half = D_MODEL // 2
        gate = (jnp.dot(lo, wg_b[:half, :], preferred_element_type=F32)
                + jnp.dot(hi, wg_b[half:, :], preferred_element_type=F32))
        up = (jnp.dot(lo, wu_b[:half, :], preferred_element_type=F32)
              + jnp.dot(hi, wu_b[half:, :], preferred_element_type=F32))
        h_ref[...] = (gate * _sigmoid(gate) * up).astype(BF16)


def _experts_down_kernel(be_ref, nv_ref, run_ref, nxt_ref, h_ref, wd_hbm, o_ref, wd_st, sem, wd_b, *, layer):
    _expert_weights(layer, be_ref, run_ref, nxt_ref, (wd_hbm,), (wd_st,), sem, (wd_b,))

    @pl.when(pl.program_id(0) >= nv_ref[0])
    def _():
        o_ref[...] = jnp.zeros_like(o_ref)

    @pl.when(pl.program_id(0) < nv_ref[0])
    def _():
        _store_token_tiles(o_ref, jnp.dot(h_ref[...], wd_b[...], preferred_element_type=F32))


def _experts(layer, sched, xs, w_gate, w_up, w_down):
    rows = lambda c: pl.BlockSpec((EXPERT_TM, c), lambda i, *_: (i, 0))
    tiles = pl.BlockSpec((EXPERT_TM * ROW_CHUNKS, LANES), lambda i, *_: (i, 0))
    any_spec = pl.BlockSpec(memory_space=pl.ANY)
    up_shape, down_shape = (D_MODEL, D_EXPERT), (D_EXPERT, D_MODEL)
    h = pl.pallas_call(
        functools.partial(_experts_up_kernel, layer=layer),
        grid_spec=pltpu.PrefetchScalarGridSpec(
            num_scalar_prefetch=4,
            grid=(N_EXPERT_BLOCKS,),
            in_specs=[tiles, any_spec, any_spec],
            out_specs=rows(D_EXPERT),
            scratch_shapes=[pltpu.VMEM((2,) + up_shape, F32), pltpu.VMEM((2,) + up_shape, F32),
                            pltpu.SemaphoreType.DMA((2,)),
                            pltpu.VMEM(up_shape, BF16), pltpu.VMEM(up_shape, BF16)],
        ),
        out_shape=jax.ShapeDtypeStruct((N_PAD, D_EXPERT), BF16),
        compiler_params=_cparams("arbitrary"),
        name="experts_up",
    )(*sched, xs, w_gate, w_up)
    return pl.pallas_call(
        functools.partial(_experts_down_kernel, layer=layer),
        grid_spec=pltpu.PrefetchScalarGridSpec(
            num_scalar_prefetch=4,
            grid=(N_EXPERT_BLOCKS,),
            in_specs=[rows(D_EXPERT), any_spec],
            out_specs=pl.BlockSpec((EXPERT_TM * OUT_CHUNKS, LANES), lambda i, *_: (i, 0)),
            scratch_shapes=[pltpu.VMEM((2,) + down_shape, F32), pltpu.SemaphoreType.DMA((2,)),
                            pltpu.VMEM(down_shape, BF16)],
        ),
        out_shape=jax.ShapeDtypeStruct((N_PAD * OUT_CHUNKS, LANES), F32),
        compiler_params=_cparams("arbitrary"),
        name="experts_down",
    )(*sched, h, w_down)


def _final_kernel(dest, ys_hbm, rw_ref, x1_ref, m_ref, lng_ref, lnb_ref, *rest, emit_next):
    if emit_next:
        mn_ref, x2_ref, un_ref, buf, sem, un_scr = rest
    else:
        x2_ref, buf, sem = rest
    i = pl.program_id(0)
    n = pl.num_programs(0)

    def row_dma(tile, slot, r, k):
        d = dest[k * N_TOK + tile * MIX_TM + r]
        src = ys_hbm.at[pl.ds(pl.multiple_of(d * OUT_CHUNKS, OUT_CHUNKS), OUT_CHUNKS)]
        dst = buf.at[slot, k, pl.ds(pl.multiple_of(r * OUT_CHUNKS, OUT_CHUNKS), OUT_CHUNKS)]
        return pltpu.make_async_copy(src, dst, sem.at[slot])

    def gather(tile, slot):
        def issue(r, c):
            for k in range(TOP_K):
                row_dma(tile, slot, r, k).start(priority=k)
            return c
        lax.fori_loop(0, MIX_TM, issue, 0, unroll=8)

    @pl.when(i == 0)
    def _():
        gather(0, 0)

    @pl.when(i + 1 < n)
    def _():
        gather(i + 1, (i + 1) % 2)

    slot = i % 2

    def drain(r, c):
        for k in range(TOP_K):
            row_dma(0, slot, 0, k).wait()
        return c
    lax.fori_loop(0, MIX_TM, drain, 0, unroll=8)

    expert_rows = lambda k: _load_token_tiles(buf.at[slot, k], MIX_TM)
    m = m_ref[0]
    g2 = m[:, 5 * D_MODEL:6 * D_MODEL]
    f = rw_ref[:, 0:1] * expert_rows(0) + rw_ref[:, 1:2] * expert_rows(1)
    x2 = _ln(ALPHA * x1_ref[...] + g2 * f) * lng_ref[1:2, :] + lnb_ref[1:2, :]
    x2_ref[...] = x2
    if emit_next:
        mn = mn_ref[0]
        un = _ln(x2) * (1.0 + mn[:, D_MODEL:2 * D_MODEL]) + mn[:, :D_MODEL]
        for c in range(OUT_CHUNKS):
            un_scr[c] = un[:, c * LANES:(c + 1) * LANES]
        for n2 in range(DFT_RADIX):
            rows = pl.ds(n2, MIX_TM // DFT_RADIX, stride=DFT_RADIX)
            un_ref[:, n2 * D_MODEL:(n2 + 1) * D_MODEL] = jnp.concatenate(
                [un_scr[c, rows, :] for c in range(OUT_CHUNKS)], axis=1).astype(BF16)


def _final(dest, ys, rw_t, x1, mods3, ln_g, ln_b, mods3_next=None):
    tiles_per_seq = SEQ // MIX_TM
    emit_next = mods3_next is not None
    mspec = pl.BlockSpec((1, 1, 6 * D_MODEL), lambda i, d: (i // tiles_per_seq, 0, 0))
    row = pl.BlockSpec((MIX_TM, D_MODEL), lambda i, d: (i, 0))
    in_specs = [pl.BlockSpec(memory_space=pl.ANY),
                pl.BlockSpec((MIX_TM, TOP_K), lambda i, d: (i, 0)),
                row, mspec,
                pl.BlockSpec((2, D_MODEL), lambda i, d: (0, 0)),
                pl.BlockSpec((2, D_MODEL), lambda i, d: (0, 0))]
    args = [ys, rw_t, x1, mods3, ln_g, ln_b]
    out_specs = [row]
    out_shape = [jax.ShapeDtypeStruct((N_TOK, D_MODEL), F32)]
    scratch = [pltpu.VMEM((2, TOP_K, MIX_TM * OUT_CHUNKS, LANES), F32), pltpu.SemaphoreType.DMA((2,))]
    if emit_next:
        in_specs.append(mspec)
        args.append(mods3_next)
        out_specs.append(pl.BlockSpec((MIX_TM // DFT_RADIX, DFT_RADIX * D_MODEL), lambda i, d: (i, 0)))
        out_shape.append(jax.ShapeDtypeStruct((N_TOK // DFT_RADIX, DFT_RADIX * D_MODEL), BF16))
        scratch.append(pltpu.VMEM((OUT_CHUNKS, MIX_TM, LANES), F32))
    return pl.pallas_call(
        functools.partial(_final_kernel, emit_next=emit_next),
        grid_spec=pltpu.PrefetchScalarGridSpec(
            num_scalar_prefetch=1,
            grid=(N_TOK // MIX_TM,),
            in_specs=in_specs,
            out_specs=out_specs,
            scratch_shapes=scratch,
        ),
        out_shape=out_shape,
        compiler_params=_cparams("arbitrary"),
        name="final",
    )(dest, *args)


def _moe(layer, u2p, route_e, route_w, rank, cnt, w_gate, w_up, w_down):
    n_tiles = N_TOK // MIX_TM
    cnt = cnt[:, :, 0]
    counts = jnp.sum(cnt, axis=0)
    padded = (counts + EXPERT_TM - 1) // EXPERT_TM * EXPERT_TM
    pad_end = jnp.cumsum(padded)
    pad_start = pad_end - padded
    base = pad_start[None, :] + jnp.cumsum(cnt, axis=0) - cnt
    e_r = route_e.reshape(TOP_K, n_tiles, MIX_TM)
    onehot = e_r[..., None] == jnp.arange(N_EXPERTS, dtype=jnp.int32)
    dest = jnp.sum(jnp.where(onehot, base[None, :, None, :], 0), axis=-1).reshape(TOP_K, N_TOK) + rank
    dest = dest.reshape(-1).astype(jnp.int32)
    n_valid = (pad_end[-1] // EXPERT_TM).astype(jnp.int32).reshape(1)
    block_row0 = jnp.arange(N_EXPERT_BLOCKS, dtype=jnp.int32) * EXPERT_TM
    present = padded > 0
    expert_ids = jnp.arange(N_EXPERTS, dtype=jnp.int32)
    last_expert = jnp.max(jnp.where(present, expert_ids, 0))
    block_expert = jnp.minimum(
        jnp.sum(pad_end[None, :] <= block_row0[:, None], axis=1), last_expert).astype(jnp.int32)
    run_of_expert = jnp.cumsum(present) - present
    block_run = jnp.sum(
        jnp.where(expert_ids[None, :] == block_expert[:, None], run_of_expert[None, :], 0), axis=1)
    later = present[None, :] & (expert_ids[None, :] > block_expert[:, None])
    block_next = jnp.min(jnp.where(later, expert_ids[None, :], N_EXPERTS), axis=1)
    block_next = jnp.where(block_next == N_EXPERTS, -1, block_next)
    sched = (block_expert, n_valid, block_run.astype(jnp.int32), block_next.astype(jnp.int32))

    fill_start = jnp.concatenate([pad_start + counts, pad_end[-1:]]).astype(jnp.int32)
    fill_n = jnp.concatenate([padded - counts, N_PAD - pad_end[-1:]]).astype(jnp.int32)
    xs = _dispatch(u2p, dest, fill_start, fill_n)
    ys = _experts(layer, sched, xs, w_gate, w_up, w_down)
    return dest, ys, route_w.T


def _dft_tables():
    dft = lambda n: 2 * np.pi * (np.outer(np.arange(n), np.arange(n)) % n) / n
    ang = dft(DFT_SUB)
    cs = jnp.asarray(np.stack([np.cos(ang), np.sin(ang)]) * SEQ ** -0.5, F32).astype(BF16)
    k1 = np.arange(DFT_SUB, dtype=np.float64)
    tw = np.zeros((DFT_SUB, 2 * DFT_RADIX), np.float64)
    for n2 in range(1, DFT_RADIX):
        tw[:, 2 * n2 - 2] = np.cos(2 * np.pi * n2 * k1 / SEQ)
        tw[:, 2 * n2 - 1] = np.sin(2 * np.pi * n2 * k1 / SEQ)
    ang = dft(FOURIER_CH)
    cc = jnp.asarray(np.cos(ang) * FOURIER_CH ** -0.5, F32).astype(BF16)
    sc = jnp.asarray(np.sin(ang) * FOURIER_CH ** -0.5, F32).astype(BF16)
    return cs, jnp.asarray(tw, F32), cc, sc


def kernel(x, c, ctx, c_ctx, ada_w, ada_b, w_mix_in, rpb, conv_w, w_mix_out, w_fourier_out,
           router_w, router_b, w_gate, w_up, w_down, ln_g, ln_b):
    x2d = x.reshape(N_TOK, D_MODEL)
    c8 = jnp.concatenate(
        [c, c_ctx[None, :], jnp.zeros((SUBLANES - BATCH - 1, D_MODEL), F32)], axis=0)
    mods = _mods(c8, ada_w, ada_b)
    mods3 = [mods[l].reshape(SUBLANES, 1, 6 * D_MODEL) for l in range(DEPTH)]
    perm = lambda a: a.reshape(N_GROUPS, EXPERTS_PER_GROUP, -1).transpose(1, 0, 2).reshape(N_EXPERTS, -1)
    rw = jnp.pad(perm(router_w.T).T, ((0, 0), (0, LANES - N_EXPERTS)))
    rw_hi = rw.astype(BF16)
    rw_lo = (rw - rw_hi.astype(F32)).astype(BF16)
    rb = perm(router_b.reshape(N_EXPERTS, 1))
    tiles_per_seq = SEQ // PROJ_TM

    w_in = w_mix_in[0].astype(BF16)
    proj = _ln_proj(x2d, mods3[0], w_in, 0, MIX_IN_WIDTH, lambda i: i // tiles_per_seq)
    kvc = _ln_proj(ctx.reshape(BATCH * CTX_LEN, D_MODEL), mods3[0], w_in,
                   ATTN_WIDTH // PROJ_TN, 2 * ATTN_WIDTH, lambda i: BATCH)
    attn = _natten(proj, kvc, _bias_table(rpb[0]))
    x1, u2p, route_e, route_w, rank, cnt = _mix_out(
        attn, proj, conv_w[0], w_mix_out[0].astype(BF16), x2d, mods3[0], ln_g[0], ln_b[0],
        rw_hi, rw_lo, rb)
    dest, ys, rw_t = _moe(0, u2p, route_e, route_w, rank, cnt, w_gate, w_up, w_down)
    x2, u_next = _final(dest, ys, rw_t, x1, mods3[0], ln_g[0], ln_b[0], mods3_next=mods3[1])

    cs, tw, cc, sc = _dft_tables()
    pq = _dft_seq(cs, tw, u_next)
    x1, u2p, route_e, route_w, rank, cnt = _four_out(
        pq, cc, sc, w_fourier_out[0].astype(BF16), x2, mods3[1], ln_g[1], ln_b[1],
        rw_hi, rw_lo, rb)
    dest, ys, rw_t = _moe(1, u2p, route_e, route_w, rank, cnt, w_gate, w_up, w_down)
    (x3,) = _final(dest, ys, rw_t, x1, mods3[1], ln_g[1], ln_b[1])
    return x3.reshape(BATCH, SEQ, D_MODEL)
```

```python
import functools

import numpy as np
import jax
import jax.numpy as jnp
from jax import lax
from jax.experimental import pallas as pl
from jax.experimental.pallas import tpu as pltpu

D_MODEL = 2048
BATCH = 4
SEQ = 4096
DEPTH = 2
GRID_W = 64
GRID_ROWS = SEQ // GRID_W
CTX_LEN = 256
HEAD_DIM = 128
ATTN_WIDTH = D_MODEL // 2
N_HEADS = ATTN_WIDTH // HEAD_DIM
WIN_H = 8
WIN_W = 16
CONV_CH = D_MODEL // 2
MIX_IN_WIDTH = 3 * ATTN_WIDTH + 3 * CONV_CH
N_FOURIER_GROUPS = 4
FOURIER_CH = D_MODEL // N_FOURIER_GROUPS
N_EXPERTS = 32
N_GROUPS = 8
EXPERTS_PER_GROUP = N_EXPERTS // N_GROUPS
TOP_K = 2
D_EXPERT = D_MODEL // 2
ALPHA = (2 * DEPTH) ** 0.25
LN_EPS = 1e-6
NEG_INF = -1e30
N_TOK = BATCH * SEQ

V7X_VMEM_LIMIT_BYTES = 56 * 1024 * 1024
SUBLANES = 8
LANES = 128

PROJ_TM = 512
PROJ_TN = 1024
ATT_QROWS = 4
ATT_KROWS = 12
ATT_Q = ATT_QROWS * GRID_W
ATT_K = ATT_KROWS * GRID_W
N_QBLOCKS = GRID_ROWS // ATT_QROWS
MIX_TM = 256
DFT_RADIX = 4
DFT_SUB = SEQ // DFT_RADIX
DFT_TK = 256
DFT_TN = 1024
EXPERT_TM = 256
N_ROWS = N_TOK * TOP_K
N_EXPERT_BLOCKS = N_ROWS // EXPERT_TM + N_EXPERTS
N_PAD = N_EXPERT_BLOCKS * EXPERT_TM
DISPATCH_TM = 1024
WEIGHT_DMA_CHUNKS = 4
CAST_ROWS = 64
N_FILL_SEGMENTS = N_EXPERTS + 1
ROW_CHUNKS = D_MODEL // 2 // LANES
OUT_CHUNKS = D_MODEL // LANES
MIX_BLOCK = 512
MIX_SUBTILES = MIX_BLOCK // MIX_TM
MODS_TN = 1024

F32 = jnp.float32
BF16 = jnp.bfloat16


def _cparams(*sem):
    return pltpu.CompilerParams(dimension_semantics=sem, vmem_limit_bytes=V7X_VMEM_LIMIT_BYTES)


def _ln(x):
    mu = jnp.mean(x, axis=-1, keepdims=True)
    xc = x - mu
    var = jnp.mean(xc * xc, axis=-1, keepdims=True)
    return xc * lax.rsqrt(var + LN_EPS)


def _sigmoid(x):
    return 1.0 / (1.0 + jnp.exp(-x))


def _mods_kernel(c_ref, w_ref, b_ref, o_ref):
    c = c_ref[...]
    s = (c * _sigmoid(c)).astype(BF16)
    o_ref[0] = jnp.dot(s, w_ref[0].astype(BF16), preferred_element_type=F32) + b_ref[0]


def _mods(c8, ada_w, ada_b):
    n = 6 * D_MODEL
    return pl.pallas_call(
        _mods_kernel,
        grid=(DEPTH, n // MODS_TN),
        in_specs=[
            pl.BlockSpec((SUBLANES, D_MODEL), lambda l, j: (0, 0)),
            pl.BlockSpec((1, D_MODEL, MODS_TN), lambda l, j: (l, 0, j)),
            pl.BlockSpec((1, 1, MODS_TN), lambda l, j: (l, 0, j)),
        ],
        out_specs=pl.BlockSpec((1, SUBLANES, MODS_TN), lambda l, j: (l, 0, j)),
        out_shape=jax.ShapeDtypeStruct((DEPTH, SUBLANES, n), F32),
        compiler_params=_cparams("arbitrary", "arbitrary"),
        name="mods",
    )(c8, ada_w, ada_b.reshape(DEPTH, 1, n))


def _ln_proj_kernel(x_ref, m_ref, w_ref, o_ref, u_scr):
    @pl.when(pl.program_id(1) == 0)
    def _():
        m = m_ref[0]
        y = _ln(x_ref[...])
        u_scr[...] = (y * (1.0 + m[:, D_MODEL:2 * D_MODEL]) + m[:, :D_MODEL]).astype(BF16)

    o_ref[...] = jnp.dot(u_scr[...], w_ref[...], preferred_element_type=F32).astype(o_ref.dtype)


def _ln_proj(x2d, mods3, w, col_block0, n_cols, mod_row_of_tile):
    n_rows = x2d.shape[0]
    return pl.pallas_call(
        _ln_proj_kernel,
        grid=(n_rows // PROJ_TM, n_cols // PROJ_TN),
        in_specs=[
            pl.BlockSpec((PROJ_TM, D_MODEL), lambda i, j: (i, 0)),
            pl.BlockSpec((1, 1, 6 * D_MODEL), lambda i, j: (mod_row_of_tile(i), 0, 0)),
            pl.BlockSpec((D_MODEL, PROJ_TN), lambda i, j: (0, col_block0 + j)),
        ],
        out_specs=pl.BlockSpec((PROJ_TM, PROJ_TN), lambda i, j: (i, j)),
        out_shape=jax.ShapeDtypeStruct((n_rows, n_cols), BF16),
        scratch_shapes=[pltpu.VMEM((PROJ_TM, D_MODEL), BF16)],
        compiler_params=_cparams("arbitrary", "arbitrary"),
        name="ln_proj",
    )(x2d, mods3, w)


def _bias_table_kernel(rpb_ref, o_ref):
    h = pl.program_id(0)
    qc = lax.broadcasted_iota(jnp.int32, (GRID_W, GRID_W), 0)
    kc = lax.broadcasted_iota(jnp.int32, (GRID_W, GRID_W), 1)
    col_start = jnp.clip(qc - WIN_W // 2, 0, GRID_W - WIN_W)
    col_ok = (kc >= col_start) & (kc < col_start + WIN_W)
    col_idx = jnp.clip(kc - qc, -(WIN_W - 1), WIN_W - 1) + WIN_W - 1
    n_rb = 2 * WIN_H - 1
    n_cb = 2 * WIN_W - 1
    tiles = []
    for ri in range(n_rb):
        acc = jnp.zeros((GRID_W, GRID_W), F32)
        for i in range(n_cb):
            acc = jnp.where(col_idx == i, rpb_ref[(h * n_rb + ri) * n_cb + i], acc)
        tiles.append(jnp.where(col_ok, acc, NEG_INF))
    masked = jnp.full((GRID_W, GRID_W), NEG_INF, F32)
    for typ, blk in enumerate((0, 1, N_QBLOCKS - 1)):
        ks = min(max(blk * ATT_QROWS - WIN_H // 2, 0), GRID_ROWS - ATT_KROWS)
        for a in range(ATT_QROWS):
            r = blk * ATT_QROWS + a
            rs = min(max(r - WIN_H // 2, 0), GRID_ROWS - WIN_H)
            strip = [tiles[ks + c - r + WIN_H - 1] if rs <= ks + c < rs + WIN_H else masked
                     for c in range(ATT_KROWS)]
            o_ref[0, typ, a * GRID_W:(a + 1) * GRID_W, :] = jnp.concatenate(strip, axis=1)


def _bias_table(rpb_l):
    flat = rpb_l.reshape(-1).astype(F32)
    return pl.pallas_call(
        _bias_table_kernel,
        grid=(N_HEADS,),
        in_specs=[pl.BlockSpec(memory_space=pltpu.SMEM)],
        out_specs=pl.BlockSpec((1, 3, ATT_Q, ATT_K), lambda h: (h, 0, 0, 0)),
        out_shape=jax.ShapeDtypeStruct((N_HEADS, 3, ATT_Q, ATT_K), F32),
        compiler_params=_cparams("arbitrary"),
        name="bias_table",
    )(flat)


def _natten_kernel(q_ref, k_ref, v_ref, kc_ref, vc_ref, bias_ref, o_ref):
    scale = HEAD_DIM ** -0.5
    kc = kc_ref[...]
    vc = vc_ref[...]
    nt = (((1,), (1,)), ((), ()))

    def body(j, carry):
        typ = jnp.where(j == 0, 0, jnp.where(j == N_QBLOCKS - 1, 2, 1))
        q0 = pl.multiple_of(j * ATT_Q, ATT_Q)
        k0 = pl.multiple_of(
            jnp.clip(j * ATT_QROWS - WIN_H // 2, 0, GRID_ROWS - ATT_KROWS) * GRID_W, 4 * GRID_W)
        q = q_ref[pl.ds(q0, ATT_Q), :]
        kw = k_ref[pl.ds(k0, ATT_K), :]
        vw = v_ref[pl.ds(k0, ATT_K), :]
        s_loc = lax.dot_general(q, kw, nt, preferred_element_type=F32) * scale + bias_ref[typ]
        s_ctx = lax.dot_general(q, kc, nt, preferred_element_type=F32) * scale
        m = jnp.maximum(jnp.max(s_loc, axis=-1, keepdims=True), jnp.max(s_ctx, axis=-1, keepdims=True))
        p_loc = jnp.exp(s_loc - m)
        p_ctx = jnp.exp(s_ctx - m)
        denom = jnp.sum(p_loc, axis=-1, keepdims=True) + jnp.sum(p_ctx, axis=-1, keepdims=True)
        o = (jnp.dot(p_loc.astype(BF16), vw, preferred_element_type=F32)
             + jnp.dot(p_ctx.astype(BF16), vc, preferred_element_type=F32))
        o_ref[pl.ds(q0, ATT_Q), :] = (o / denom).astype(o_ref.dtype)
        return carry

    lax.fori_loop(0, N_QBLOCKS, body, 0)


def _natten(proj, kvc, bias):
    blk = lambda off: pl.BlockSpec((SEQ, HEAD_DIM), lambda b, h: (b, off + h))
    cblk = lambda off: pl.BlockSpec((CTX_LEN, HEAD_DIM), lambda b, h: (b, off + h))
    return pl.pallas_call(
        _natten_kernel,
        grid=(BATCH, N_HEADS),
        in_specs=[blk(0), blk(N_HEADS), blk(2 * N_HEADS), cblk(0), cblk(N_HEADS),
                  pl.BlockSpec((None, 3, ATT_Q, ATT_K), lambda b, h: (h, 0, 0, 0))],
        out_specs=pl.BlockSpec((SEQ, HEAD_DIM), lambda b, h: (b, h)),
        out_shape=jax.ShapeDtypeStruct((N_TOK, ATTN_WIDTH), BF16),
        compiler_params=_cparams("arbitrary", "arbitrary"),
        name="natten",
    )(proj, proj, proj, kvc, kvc, bias)


def _pack_bf16_pair(lo, hi):
    return pltpu.pack_elementwise([lo, hi], packed_dtype=BF16)


def _unpack_bf16_pair(w):
    unpack = lambda i: pltpu.unpack_elementwise(w, index=i, packed_dtype=BF16, unpacked_dtype=F32)
    return unpack(0), unpack(1)


def _store_token_tiles(ref, vals):
    tm, n = vals.shape[0], vals.shape[1] // LANES
    for c in range(n):
        ref[pl.ds(c, tm, stride=n), :] = vals[:, c * LANES:(c + 1) * LANES]


def _load_token_tiles(ref, tm):
    n = ref.shape[0] // tm
    return jnp.concatenate([ref[pl.ds(c, tm, stride=n), :] for c in range(n)], axis=1)


def _epilogue(y, s, x_ref, m_ref, lng_ref, lnb_ref, rwh_ref, rwl_ref, rb_ref,
              x1_ref, u2_ref, re_ref, rw_ref, rk_ref, cnt_ref):
    tm = MIX_TM
    rows = slice(s * tm, (s + 1) * tm)
    m = m_ref[0]
    g1 = m[:, 2 * D_MODEL:3 * D_MODEL]
    sh2 = m[:, 3 * D_MODEL:4 * D_MODEL]
    sc2 = m[:, 4 * D_MODEL:5 * D_MODEL]
    x1 = _ln(ALPHA * x_ref[rows, :] + g1 * y) * lng_ref[0:1, :] + lnb_ref[0:1, :]
    x1_ref[rows, :] = x1
    u2 = _ln(x1) * (1.0 + sc2) + sh2
    _store_token_tiles(u2_ref.at[pl.ds(s * tm * ROW_CHUNKS, tm * ROW_CHUNKS)],
                       _pack_bf16_pair(u2[:, :D_MODEL // 2], u2[:, D_MODEL // 2:]))

    u_hi = u2.astype(BF16)
    u_lo = (u2 - u_hi.astype(F32)).astype(BF16)
    logits = (jnp.dot(u_hi, rwh_ref[...], preferred_element_type=F32)
              + jnp.dot(u_lo, rwh_ref[...], preferred_element_type=F32)
              + jnp.dot(u_hi, rwl_ref[...], preferred_element_type=F32))
    aff = _sigmoid(logits.T[:N_EXPERTS, :])
    biased = aff + rb_ref[...]
    a = [biased[l * N_GROUPS:(l + 1) * N_GROUPS, :] for l in range(EXPERTS_PER_GROUP)]
    f = [aff[l * N_GROUPS:(l + 1) * N_GROUPS, :] for l in range(EXPERTS_PER_GROUP)]
    hi01, lo01 = jnp.maximum(a[0], a[1]), jnp.minimum(a[0], a[1])
    hi23, lo23 = jnp.maximum(a[2], a[3]), jnp.minimum(a[2], a[3])
    top1 = jnp.maximum(hi01, hi23)
    top2 = jnp.maximum(jnp.minimum(hi01, hi23), jnp.maximum(lo01, lo23))
    gscore = top1 + top2
    gi = lax.broadcasted_iota(jnp.int32, (N_GROUPS, tm), 0)
    gmax = jnp.max(gscore, axis=0, keepdims=True)
    group = jnp.min(jnp.where(gscore == gmax, gi, N_GROUPS), axis=0, keepdims=True)
    sel = gi == group
    b = [jnp.sum(jnp.where(sel, a[l], 0.0), axis=0, keepdims=True) for l in range(EXPERTS_PER_GROUP)]
    c = [jnp.sum(jnp.where(sel, f[l], 0.0), axis=0, keepdims=True) for l in range(EXPERTS_PER_GROUP)]

    def first_argmax(vals):
        mx = functools.reduce(jnp.maximum, vals)
        idx = jnp.full(mx.shape, len(vals) - 1, jnp.int32)
        for l in range(len(vals) - 2, -1, -1):
            idx = jnp.where(vals[l] == mx, l, idx)
        return idx

    i1 = first_argmax(b)
    i2 = first_argmax([jnp.where(i1 == l, -jnp.inf, b[l]) for l in range(EXPERTS_PER_GROUP)])
    pick = lambda idx: functools.reduce(
        lambda acc, l: jnp.where(idx == l, c[l], acc), range(1, EXPERTS_PER_GROUP), c[0])
    w1, w2 = pick(i1), pick(i2)
    wsum = w1 + w2
    e1 = group * EXPERTS_PER_GROUP + i1
    e2 = group * EXPERTS_PER_GROUP + i2
    re_ref[0:1, rows] = e1
    re_ref[1:2, rows] = e2
    rw_ref[0:1, rows] = w1 / wsum
    rw_ref[1:2, rows] = w2 / wsum

    ei = lax.broadcasted_iota(jnp.int32, (N_EXPERTS, tm), 0)
    oh1 = ei == e1
    oh2 = ei == e2
    oh = jnp.where(oh1 | oh2, 1.0, 0.0)
    t_row = lax.broadcasted_iota(jnp.int32, (tm, tm), 0)
    t_col = lax.broadcasted_iota(jnp.int32, (tm, tm), 1)
    before = jnp.where(t_row < t_col, 1.0, 0.0).astype(BF16)
    prefix = jnp.dot(oh.astype(BF16), before, preferred_element_type=F32)
    rk_ref[0:1, rows] = jnp.sum(jnp.where(oh1, prefix, 0.0), axis=0, keepdims=True).astype(jnp.int32)
    rk_ref[1:2, rows] = jnp.sum(jnp.where(oh2, prefix, 0.0), axis=0, keepdims=True).astype(jnp.int32)
    cnt = jnp.sum(oh, axis=1, keepdims=True)
    cnt_ref[s] = jnp.broadcast_to(cnt, (N_EXPERTS, LANES)).astype(jnp.int32)


def _resident(shape):
    return pl.BlockSpec(shape, lambda i: (0,) * len(shape), pipeline_mode=pl.Buffered(1))


def _epilogue_specs():
    blocks_per_seq = SEQ // MIX_BLOCK
    n_tiles = N_TOK // MIX_TM
    in_specs = [
        pl.BlockSpec((MIX_BLOCK, D_MODEL), lambda i: (i, 0)),
        pl.BlockSpec((1, 1, 6 * D_MODEL), lambda i: (i // blocks_per_seq, 0, 0)),
        _resident((2, D_MODEL)),
        _resident((2, D_MODEL)),
        _resident((D_MODEL, LANES)),
        _resident((D_MODEL, LANES)),
        _resident((N_EXPERTS, 1)),
    ]
    out_specs = [
        pl.BlockSpec((MIX_BLOCK, D_MODEL), lambda i: (i, 0)),
        pl.BlockSpec((MIX_BLOCK * ROW_CHUNKS, LANES), lambda i: (i, 0)),
        pl.BlockSpec((TOP_K, MIX_BLOCK), lambda i: (0, i)),
        pl.BlockSpec((TOP_K, MIX_BLOCK), lambda i: (0, i)),
        pl.BlockSpec((TOP_K, MIX_BLOCK), lambda i: (0, i)),
        pl.BlockSpec((MIX_SUBTILES, N_EXPERTS, LANES), lambda i: (i, 0, 0)),
    ]
    out_shape = [
        jax.ShapeDtypeStruct((N_TOK, D_MODEL), F32),
        jax.ShapeDtypeStruct((N_TOK * ROW_CHUNKS, LANES), jnp.uint32),
        jax.ShapeDtypeStruct((TOP_K, N_TOK), jnp.int32),
        jax.ShapeDtypeStruct((TOP_K, N_TOK), F32),
        jax.ShapeDtypeStruct((TOP_K, N_TOK), jnp.int32),
        jax.ShapeDtypeStruct((n_tiles, N_EXPERTS, LANES), jnp.int32),
    ]
    return in_specs, out_specs, out_shape


def _mix_out_kernel(attn_ref, gb_ref, gc_ref, xin_ref, gcp_ref, xinp_ref, gcn_ref, xinn_ref,
                    cw_ref, wo_ref, *epilogue_refs):
    i = pl.program_id(0)
    blocks_per_seq = SEQ // MIX_BLOCK
    z = gc_ref[...].astype(F32) * xin_ref[...].astype(F32)
    halo = 2 * SUBLANES
    zp_row = (gcp_ref[...].astype(F32) * xinp_ref[...].astype(F32))[halo - 1:halo, :]
    zn_row = (gcn_ref[...].astype(F32) * xinn_ref[...].astype(F32))[0:1, :]
    zp_row = jnp.where(i % blocks_per_seq == 0, 0.0, zp_row)
    zn_row = jnp.where(i % blocks_per_seq == blocks_per_seq - 1, 0.0, zn_row)
    row = lax.broadcasted_iota(jnp.int32, (MIX_BLOCK, 1), 0)
    z_prev = jnp.where(row == 0, zp_row, pltpu.roll(z, 1, axis=0))
    z_next = jnp.where(row == MIX_BLOCK - 1, zn_row, pltpu.roll(z, MIX_BLOCK - 1, axis=0))
    conv = cw_ref[0:1, :] * z_prev + cw_ref[1:2, :] * z + cw_ref[2:3, :] * z_next
    gated = (gb_ref[...].astype(F32) * conv).astype(BF16)
    y = (jnp.dot(attn_ref[...], wo_ref[:ATTN_WIDTH, :], preferred_element_type=F32)
         + jnp.dot(gated, wo_ref[ATTN_WIDTH:, :], preferred_element_type=F32))
    for s in range(MIX_SUBTILES):
        _epilogue(y[s * MIX_TM:(s + 1) * MIX_TM, :], s, *epilogue_refs)


def _mix_out(attn, proj, conv_w, w_out, x2d, mods3, ln_g, ln_b, rw_hi, rw_lo, rb):
    halo = 2 * SUBLANES
    hb = MIX_BLOCK // halo
    n_halo_blocks = N_TOK // halo
    cblk = lambda off: pl.BlockSpec((MIX_BLOCK, CONV_CH), lambda i: (i, off))
    prev = lambda off: pl.BlockSpec((halo, CONV_CH), lambda i: (jnp.maximum(i * hb - 1, 0), off))
    nxt = lambda off: pl.BlockSpec(
        (halo, CONV_CH), lambda i: (jnp.minimum((i + 1) * hb, n_halo_blocks - 1), off))
    ep_in, out_specs, out_shape = _epilogue_specs()
    return pl.pallas_call(
        _mix_out_kernel,
        grid=(N_TOK // MIX_BLOCK,),
        in_specs=[pl.BlockSpec((MIX_BLOCK, ATTN_WIDTH), lambda i: (i, 0)),
                  cblk(3), cblk(4), cblk(5), prev(4), prev(5), nxt(4), nxt(5),
                  _resident((3, CONV_CH)), _resident((D_MODEL, D_MODEL))] + ep_in,
        out_specs=out_specs,
        out_shape=out_shape,
        compiler_params=_cparams("arbitrary"),
        name="mix_out",
    )(attn, proj, proj, proj, proj, proj, proj, proj, conv_w, w_out, x2d, mods3, ln_g, ln_b,
      rw_hi, rw_lo, rb)


def _dft_seq_kernel(cs_ref, tw_ref, u0_ref, u1_ref, u2_ref, u3_ref, o_ref):
    c_m, s_m = cs_ref[0], cs_ref[1]
    tr, ti = [], []
    for n2, u_ref in enumerate((u0_ref, u1_ref, u2_ref, u3_ref)):
        x = u_ref[...]
        a = jnp.dot(c_m, x, preferred_element_type=F32)
        b = jnp.dot(s_m, x, preferred_element_type=F32)
        if n2 == 0:
            tr.append(a)
            ti.append(b)
        else:
            c = tw_ref[:, 2 * n2 - 2:2 * n2 - 1]
            s = tw_ref[:, 2 * n2 - 1:2 * n2]
            tr.append(a * c - b * s)
            ti.append(a * s + b * c)
    o_ref[0, 0] = (tr[0] + tr[1] + tr[2] + tr[3]).astype(o_ref.dtype)
    o_ref[1, 0] = (ti[0] + ti[1] + ti[2] + ti[3]).astype(o_ref.dtype)
    o_ref[0, 1] = (tr[0] - ti[1] - tr[2] + ti[3]).astype(o_ref.dtype)
    o_ref[1, 1] = (ti[0] + tr[1] - ti[2] - tr[3]).astype(o_ref.dtype)
    o_ref[0, 2] = (tr[0] - tr[1] + tr[2] - tr[3]).astype(o_ref.dtype)
    o_ref[1, 2] = (ti[0] - ti[1] + ti[2] - ti[3]).astype(o_ref.dtype)
    o_ref[0, 3] = (tr[0] + ti[1] - tr[2] - ti[3]).astype(o_ref.dtype)
    o_ref[1, 3] = (ti[0] - tr[1] - ti[2] + tr[3]).astype(o_ref.dtype)


def _dft_seq(cs, tw, u4):
    n_col = D_MODEL // DFT_TN
    sub = lambda n2: pl.BlockSpec((DFT_SUB, DFT_TN), lambda b, n, i: (b, n2 * n_col + n))
    out = pl.pallas_call(
        _dft_seq_kernel,
        grid=(BATCH, n_col, DFT_SUB // DFT_TK),
        in_specs=[pl.BlockSpec((2, DFT_TK, DFT_SUB), lambda b, n, i: (0, i, 0)),
                  pl.BlockSpec((DFT_TK, 2 * DFT_RADIX), lambda b, n, i: (i, 0)),
                  sub(0), sub(1), sub(2), sub(3)],
        out_specs=pl.BlockSpec((None, 2, DFT_RADIX, DFT_TK, DFT_TN), lambda b, n, i: (b, 0, 0, i, n)),
        out_shape=jax.ShapeDtypeStruct((BATCH, 2, DFT_RADIX, DFT_SUB, D_MODEL), BF16),
        compiler_params=_cparams("arbitrary", "arbitrary", "arbitrary"),
        name="dft_seq",
    )(cs, tw, u4, u4, u4, u4)
    return out.reshape(BATCH * 2 * SEQ, D_MODEL)


def _four_out_kernel(p_ref, q_ref, cc_ref, sc_ref, wf_ref, *epilogue_refs):
    y = jnp.zeros((MIX_BLOCK, D_MODEL), F32)
    for g in range(N_FOURIER_GROUPS):
        cols = slice(g * FOURIER_CH, (g + 1) * FOURIER_CH)
        fg = (jnp.dot(p_ref[:, cols], cc_ref[...], preferred_element_type=F32)
              - jnp.dot(q_ref[:, cols], sc_ref[...], preferred_element_type=F32))
        y = y + jnp.dot(fg.astype(BF16), wf_ref[cols, :], preferred_element_type=F32)
    for s in range(MIX_SUBTILES):
        _epilogue(y[s * MIX_TM:(s + 1) * MIX_TM, :], s, *epilogue_refs)


def _four_out(pq, cc, sc, wf, x2d, mods3, ln_g, ln_b, rw_hi, rw_lo, rb):
    blocks_per_seq = SEQ // MIX_BLOCK
    ep_in, out_specs, out_shape = _epilogue_specs()
    prow = lambda i: (i // blocks_per_seq) * 2 * blocks_per_seq + i % blocks_per_seq
    return pl.pallas_call(
        _four_out_kernel,
        grid=(N_TOK // MIX_BLOCK,),
        in_specs=[pl.BlockSpec((MIX_BLOCK, D_MODEL), lambda i: (prow(i), 0)),
                  pl.BlockSpec((MIX_BLOCK, D_MODEL), lambda i: (prow(i) + blocks_per_seq, 0)),
                  _resident((FOURIER_CH, FOURIER_CH)), _resident((FOURIER_CH, FOURIER_CH)),
                  _resident((D_MODEL, D_MODEL))] + ep_in,
        out_specs=out_specs,
        out_shape=out_shape,
        compiler_params=_cparams("arbitrary"),
        name="four_out",
    )(pq, pq, cc, sc, wf, x2d, mods3, ln_g, ln_b, rw_hi, rw_lo, rb)


def _dispatch_kernel(dest, fill_start, fill_n, u_ref, out_hbm, sem):
    i = pl.program_id(0)
    base = i * DISPATCH_TM

    def row_dma(r, d):
        src = u_ref.at[pl.ds(pl.multiple_of(r * ROW_CHUNKS, ROW_CHUNKS), ROW_CHUNKS)]
        dst = out_hbm.at[pl.ds(pl.multiple_of(d * ROW_CHUNKS, ROW_CHUNKS), ROW_CHUNKS)]
        return pltpu.make_async_copy(src, dst, sem)

    def issue(r, c):
        for k in range(TOP_K):
            row_dma(r, dest[k * N_TOK + base + r]).start(priority=k)
        return c

    def drain(r, c):
        for k in range(TOP_K):
            row_dma(0, 0).wait()
        return c

    lax.fori_loop(0, DISPATCH_TM, issue, 0, unroll=8)
    lax.fori_loop(0, DISPATCH_TM, drain, 0, unroll=8)

    @pl.when(i == pl.num_programs(0) - 1)
    def _():
        def per_segment(e, c):
            lax.fori_loop(0, fill_n[e], lambda r, c2: (row_dma(0, fill_start[e] + r).start(), c2)[1], 0)
            lax.fori_loop(0, fill_n[e], lambda r, c2: (row_dma(0, 0).wait(), c2)[1], 0)
            return c

        lax.fori_loop(0, N_FILL_SEGMENTS, per_segment, 0)


def _dispatch(u2p, dest, fill_start, fill_n):
    return pl.pallas_call(
        _dispatch_kernel,
        grid_spec=pltpu.PrefetchScalarGridSpec(
            num_scalar_prefetch=3,
            grid=(N_TOK // DISPATCH_TM,),
            in_specs=[pl.BlockSpec((DISPATCH_TM * ROW_CHUNKS, LANES), lambda i, *_: (i, 0))],
            out_specs=pl.BlockSpec(memory_space=pl.ANY),
            scratch_shapes=[pltpu.SemaphoreType.DMA(())],
        ),
        out_shape=jax.ShapeDtypeStruct((N_PAD * ROW_CHUNKS, LANES), jnp.uint32),
        compiler_params=_cparams("arbitrary"),
        name="dispatch",
    )(dest, fill_start, fill_n, u2p)


def _expert_weights(layer, be_ref, nxt_ref, w_hbm, stage, sem, w_bf16):
    i = pl.program_id(0)
    changed = (i == 0) | (be_ref[i] != be_ref[jnp.maximum(i - 1, 0)])

    def copies(e):
        out = []
        for w, st in zip(w_hbm, stage):
            rows = st.shape[0] // WEIGHT_DMA_CHUNKS
            for c in range(WEIGHT_DMA_CHUNKS):
                rs = pl.ds(c * rows, rows)
                out.append(pltpu.make_async_copy(w.at[layer, e, rs], st.at[rs], sem))
        return out

    @pl.when(changed)
    def _():
        @pl.when(i == 0)
        def _():
            for cp in copies(be_ref[0]):
                cp.start(priority=1)

        for cp in copies(be_ref[i]):
            cp.wait()
        for st, wb in zip(stage, w_bf16):
            def cast_rows(r, c, st=st, wb=wb):
                rs = pl.ds(pl.multiple_of(r * CAST_ROWS, CAST_ROWS), CAST_ROWS)
                wb[rs, :] = st[rs, :].astype(BF16)
                return c
            lax.fori_loop(0, st.shape[0] // CAST_ROWS, cast_rows, 0)

        @pl.when(nxt_ref[i] >= 0)
        def _():
            for cp in copies(nxt_ref[i]):
                cp.start(priority=1)


def _experts_kernel(be_ref, nv_ref, nxt_ref, x_ref, wg_hbm, wu_hbm, wd_hbm, o_ref,
                    wg_st, wu_st, wd_st, sem, wg_b, wu_b, wd_b, *, layer):
    _expert_weights(layer, be_ref, nxt_ref, (wg_hbm, wu_hbm, wd_hbm), (wg_st, wu_st, wd_st), sem,
                    (wg_b, wu_b, wd_b))

    @pl.when(pl.program_id(0) >= nv_ref[0])
    def _():
        o_ref[...] = jnp.zeros_like(o_ref)

    @pl.when(pl.program_id(0) < nv_ref[0])
    def _():
        lo, hi = _unpack_bf16_pair(_load_token_tiles(x_ref, EXPERT_TM))
        lo, hi = lo.astype(BF16), hi.astype(BF16)
        half = D_MODEL // 2
        gate = (jnp.dot(lo, wg_b[:half, :], preferred_element_type=F32)
                + jnp.dot(hi, wg_b[half:, :], preferred_element_type=F32))
        up = (jnp.dot(lo, wu_b[:half, :], preferred_element_type=F32)
              + jnp.dot(hi, wu_b[half:, :], preferred_element_type=F32))
        act = (gate * _sigmoid(gate) * up).astype(BF16)
        _store_token_tiles(o_ref, jnp.dot(act, wd_b[...], preferred_element_type=F32))


def _experts(layer, sched, xs, w_gate, w_up, w_down):
    any_spec = pl.BlockSpec(memory_space=pl.ANY)
    up_shape, down_shape = (D_MODEL, D_EXPERT), (D_EXPERT, D_MODEL)
    return pl.pallas_call(
        functools.partial(_experts_kernel, layer=layer),
        grid_spec=pltpu.PrefetchScalarGridSpec(
            num_scalar_prefetch=3,
            grid=(N_EXPERT_BLOCKS,),
            in_specs=[pl.BlockSpec((EXPERT_TM * ROW_CHUNKS, LANES), lambda i, *_: (i, 0)),
                      any_spec, any_spec, any_spec],
            out_specs=pl.BlockSpec((EXPERT_TM * OUT_CHUNKS, LANES), lambda i, *_: (i, 0)),
            scratch_shapes=[pltpu.VMEM(up_shape, F32), pltpu.VMEM(up_shape, F32),
                            pltpu.VMEM(down_shape, F32), pltpu.SemaphoreType.DMA(()),
                            pltpu.VMEM(up_shape, BF16), pltpu.VMEM(up_shape, BF16),
                            pltpu.VMEM(down_shape, BF16)],
        ),
        out_shape=jax.ShapeDtypeStruct((N_PAD * OUT_CHUNKS, LANES), F32),
        compiler_params=_cparams("arbitrary"),
        name="experts",
    )(*sched, xs, w_gate, w_up, w_down)


def _final_kernel(dest, ys_hbm, rw_ref, x1_ref, m_ref, lng_ref, lnb_ref, *rest, emit_next):
    if emit_next:
        mn_ref, x2_ref, un_ref, buf, sem, un_scr = rest
    else:
        x2_ref, buf, sem = rest
    i = pl.program_id(0)
    n = pl.num_programs(0)

    def row_dma(tile, slot, r, k):
        d = dest[k * N_TOK + tile * MIX_TM + r]
        src = ys_hbm.at[pl.ds(pl.multiple_of(d * OUT_CHUNKS, OUT_CHUNKS), OUT_CHUNKS)]
        dst = buf.at[slot, k, pl.ds(pl.multiple_of(r * OUT_CHUNKS, OUT_CHUNKS), OUT_CHUNKS)]
        return pltpu.make_async_copy(src, dst, sem.at[slot])

    def gather(tile, slot):
        def issue(r, c):
            for k in range(TOP_K):
                row_dma(tile, slot, r, k).start(priority=k)
            return c
        lax.fori_loop(0, MIX_TM, issue, 0, unroll=8)

    @pl.when(i == 0)
    def _():
        gather(0, 0)

    @pl.when(i + 1 < n)
    def _():
        gather(i + 1, (i + 1) % 2)

    slot = i % 2

    def drain(r, c):
        for k in range(TOP_K):
            row_dma(0, slot, 0, k).wait()
        return c
    lax.fori_loop(0, MIX_TM, drain, 0, unroll=8)

    expert_rows = lambda k: _load_token_tiles(buf.at[slot, k], MIX_TM)
    m = m_ref[0]
    g2 = m[:, 5 * D_MODEL:6 * D_MODEL]
    f = rw_ref[:, 0:1] * expert_rows(0) + rw_ref[:, 1:2] * expert_rows(1)
    x2 = _ln(ALPHA * x1_ref[...] + g2 * f) * lng_ref[1:2, :] + lnb_ref[1:2, :]
    x2_ref[...] = x2
    if emit_next:
        mn = mn_ref[0]
        un = _ln(x2) * (1.0 + mn[:, D_MODEL:2 * D_MODEL]) + mn[:, :D_MODEL]
        for c in range(OUT_CHUNKS):
            un_scr[c] = un[:, c * LANES:(c + 1) * LANES]
        for n2 in range(DFT_RADIX):
            rows = pl.ds(n2, MIX_TM // DFT_RADIX, stride=DFT_RADIX)
            un_ref[:, n2 * D_MODEL:(n2 + 1) * D_MODEL] = jnp.concatenate(
                [un_scr[c, rows, :] for c in range(OUT_CHUNKS)], axis=1).astype(BF16)


def _final(dest, ys, rw_t, x1, mods3, ln_g, ln_b, mods3_next=None):
    tiles_per_seq = SEQ // MIX_TM
    emit_next = mods3_next is not None
    mspec = pl.BlockSpec((1, 1, 6 * D_MODEL), lambda i, d: (i // tiles_per_seq, 0, 0))
    row = pl.BlockSpec((MIX_TM, D_MODEL), lambda i, d: (i, 0))
    in_specs = [pl.BlockSpec(memory_space=pl.ANY),
                pl.BlockSpec((MIX_TM, TOP_K), lambda i, d: (i, 0)),
                row, mspec,
                pl.BlockSpec((2, D_MODEL), lambda i, d: (0, 0)),
                pl.BlockSpec((2, D_MODEL), lambda i, d: (0, 0))]
    args = [ys, rw_t, x1, mods3, ln_g, ln_b]
    out_specs = [row]
    out_shape = [jax.ShapeDtypeStruct((N_TOK, D_MODEL), F32)]
    scratch = [pltpu.VMEM((2, TOP_K, MIX_TM * OUT_CHUNKS, LANES), F32), pltpu.SemaphoreType.DMA((2,))]
    if emit_next:
        in_specs.append(mspec)
        args.append(mods3_next)
        out_specs.append(pl.BlockSpec((MIX_TM // DFT_RADIX, DFT_RADIX * D_MODEL), lambda i, d: (i, 0)))
        out_shape.append(jax.ShapeDtypeStruct((N_TOK // DFT_RADIX, DFT_RADIX * D_MODEL), BF16))
        scratch.append(pltpu.VMEM((OUT_CHUNKS, MIX_TM, LANES), F32))
    return pl.pallas_call(
        functools.partial(_final_kernel, emit_next=emit_next),
        grid_spec=pltpu.PrefetchScalarGridSpec(
            num_scalar_prefetch=1,
            grid=(N_TOK // MIX_TM,),
            in_specs=in_specs,
            out_specs=out_specs,
            scratch_shapes=scratch,
        ),
        out_shape=out_shape,
        compiler_params=_cparams("arbitrary"),
        name="final",
    )(dest, *args)


def _moe(layer, u2p, route_e, route_w, rank, cnt, w_gate, w_up, w_down):
    n_tiles = N_TOK // MIX_TM
    cnt = cnt[:, :, 0]
    counts = jnp.sum(cnt, axis=0)
    padded = (counts + EXPERT_TM - 1) // EXPERT_TM * EXPERT_TM
    pad_end = jnp.cumsum(padded)
    pad_start = pad_end - padded
    base = pad_start[None, :] + jnp.cumsum(cnt, axis=0) - cnt
    e_r = route_e.reshape(TOP_K, n_tiles, MIX_TM)
    onehot = e_r[..., None] == jnp.arange(N_EXPERTS, dtype=jnp.int32)
    dest = jnp.sum(jnp.where(onehot, base[None, :, None, :], 0), axis=-1).reshape(TOP_K, N_TOK) + rank
    dest = dest.reshape(-1).astype(jnp.int32)
    n_valid = (pad_end[-1] // EXPERT_TM).astype(jnp.int32).reshape(1)
    block_row0 = jnp.arange(N_EXPERT_BLOCKS, dtype=jnp.int32) * EXPERT_TM
    present = padded > 0
    expert_ids = jnp.arange(N_EXPERTS, dtype=jnp.int32)
    last_expert = jnp.max(jnp.where(present, expert_ids, 0))
    block_expert = jnp.minimum(
        jnp.sum(pad_end[None, :] <= block_row0[:, None], axis=1), last_expert).astype(jnp.int32)
    later = present[None, :] & (expert_ids[None, :] > block_expert[:, None])
    block_next = jnp.min(jnp.where(later, expert_ids[None, :], N_EXPERTS), axis=1)
    block_next = jnp.where(block_next == N_EXPERTS, -1, block_next)
    sched = (block_expert, n_valid, block_next.astype(jnp.int32))

    fill_start = jnp.concatenate([pad_start + counts, pad_end[-1:]]).astype(jnp.int32)
    fill_n = jnp.concatenate([padded - counts, N_PAD - pad_end[-1:]]).astype(jnp.int32)
    xs = _dispatch(u2p, dest, fill_start, fill_n)
    ys = _experts(layer, sched, xs, w_gate, w_up, w_down)
    return dest, ys, route_w.T


def _dft_tables():
    dft = lambda n: 2 * np.pi * (np.outer(np.arange(n), np.arange(n)) % n) / n
    ang = dft(DFT_SUB)
    cs = jnp.asarray(np.stack([np.cos(ang), np.sin(ang)]) * SEQ ** -0.5, F32).astype(BF16)
    k1 = np.arange(DFT_SUB, dtype=np.float64)
    tw = np.zeros((DFT_SUB, 2 * DFT_RADIX), np.float64)
    for n2 in range(1, DFT_RADIX):
        tw[:, 2 * n2 - 2] = np.cos(2 * np.pi * n2 * k1 / SEQ)
        tw[:, 2 * n2 - 1] = np.sin(2 * np.pi * n2 * k1 / SEQ)
    ang = dft(FOURIER_CH)
    cc = jnp.asarray(np.cos(ang) * FOURIER_CH ** -0.5, F32).astype(BF16)
    sc = jnp.asarray(np.sin(ang) * FOURIER_CH ** -0.5, F32).astype(BF16)
    return cs, jnp.asarray(tw, F32), cc, sc


def kernel(x, c, ctx, c_ctx, ada_w, ada_b, w_mix_in, rpb, conv_w, w_mix_out, w_fourier_out,
           router_w, router_b, w_gate, w_up, w_down, ln_g, ln_b):
    x2d = x.reshape(N_TOK, D_MODEL)
    c8 = jnp.concatenate(
        [c, c_ctx[None, :], jnp.zeros((SUBLANES - BATCH - 1, D_MODEL), F32)], axis=0)
    mods = _mods(c8, ada_w, ada_b)
    mods3 = [mods[l].reshape(SUBLANES, 1, 6 * D_MODEL) for l in range(DEPTH)]
    perm = lambda a: a.reshape(N_GROUPS, EXPERTS_PER_GROUP, -1).transpose(1, 0, 2).reshape(N_EXPERTS, -1)
    rw = jnp.pad(perm(router_w.T).T, ((0, 0), (0, LANES - N_EXPERTS)))
    rw_hi = rw.astype(BF16)
    rw_lo = (rw - rw_hi.astype(F32)).astype(BF16)
    rb = perm(router_b.reshape(N_EXPERTS, 1))
    tiles_per_seq = SEQ // PROJ_TM

    w_in = w_mix_in[0].astype(BF16)
    proj = _ln_proj(x2d, mods3[0], w_in, 0, MIX_IN_WIDTH, lambda i: i // tiles_per_seq)
    kvc = _ln_proj(ctx.reshape(BATCH * CTX_LEN, D_MODEL), mods3[0], w_in,
                   ATTN_WIDTH // PROJ_TN, 2 * ATTN_WIDTH, lambda i: BATCH)
    attn = _natten(proj, kvc, _bias_table(rpb[0]))
    x1, u2p, route_e, route_w, rank, cnt = _mix_out(
        attn, proj, conv_w[0], w_mix_out[0].astype(BF16), x2d, mods3[0], ln_g[0], ln_b[0],
        rw_hi, rw_lo, rb)
    dest, ys, rw_t = _moe(0, u2p, route_e, route_w, rank, cnt, w_gate, w_up, w_down)
    x2, u_next = _final(dest, ys, rw_t, x1, mods3[0], ln_g[0], ln_b[0], mods3_next=mods3[1])

    cs, tw, cc, sc = _dft_tables()
    pq = _dft_seq(cs, tw, u_next)
    x1, u2p, route_e, route_w, rank, cnt = _four_out(
        pq, cc, sc, w_fourier_out[0].astype(BF16), x2, mods3[1], ln_g[1], ln_b[1],
        rw_hi, rw_lo, rb)
    dest, ys, rw_t = _moe(1, u2p, route_e, route_w, rank, cnt, w_gate, w_up, w_down)
    (x3,) = _final(dest, ys, rw_t, x1, mods3[1], ln_g[1], ln_b[1])
    return x3.reshape(BATCH, SEQ, D_MODEL)
```

```python
import functools

import numpy as np
import jax
import jax.numpy as jnp
from jax import lax
from jax.experimental import pallas as pl
from jax.experimental.pallas import tpu as pltpu

D_MODEL = 2048
BATCH = 4
SEQ = 4096
DEPTH = 2
GRID_W = 64
GRID_ROWS = SEQ // GRID_W
CTX_LEN = 256
HEAD_DIM = 128
ATTN_WIDTH = D_MODEL // 2
N_HEADS = ATTN_WIDTH // HEAD_DIM
WIN_H = 8
WIN_W = 16
CONV_CH = D_MODEL // 2
MIX_IN_WIDTH = 3 * ATTN_WIDTH + 3 * CONV_CH
N_FOURIER_GROUPS = 4
FOURIER_CH = D_MODEL // N_FOURIER_GROUPS
N_EXPERTS = 32
N_GROUPS = 8
EXPERTS_PER_GROUP = N_EXPERTS // N_GROUPS
TOP_K = 2
D_EXPERT = D_MODEL // 2
ALPHA = (2 * DEPTH) ** 0.25
LN_EPS = 1e-6
NEG_INF = -1e30
N_TOK = BATCH * SEQ

V7X_VMEM_LIMIT_BYTES = 56 * 1024 * 1024
SUBLANES = 8
LANES = 128

PROJ_TM = 512
PROJ_TN = 1024
ATT_QROWS = 4
ATT_KROWS = 12
ATT_Q = ATT_QROWS * GRID_W
ATT_K = ATT_KROWS * GRID_W
N_QBLOCKS = GRID_ROWS // ATT_QROWS
MIX_TM = 256
DFT_RADIX = 4
DFT_SUB = SEQ // DFT_RADIX
DFT_TK = 256
DFT_TN = 1024
EXPERT_TM = 256
N_ROWS = N_TOK * TOP_K
N_EXPERT_BLOCKS = N_ROWS // EXPERT_TM + N_EXPERTS
N_PAD = N_EXPERT_BLOCKS * EXPERT_TM
DISPATCH_TM = 1024
WEIGHT_DMA_CHUNKS = 4
CAST_ROWS = 64
N_FILL_SEGMENTS = N_EXPERTS + 1
ROW_CHUNKS = D_MODEL // 2 // LANES
OUT_CHUNKS = D_MODEL // LANES
MIX_BLOCK = 512
MIX_SUBTILES = MIX_BLOCK // MIX_TM
MODS_TN = 1024

F32 = jnp.float32
BF16 = jnp.bfloat16


def _cparams(*sem):
    return pltpu.CompilerParams(dimension_semantics=sem, vmem_limit_bytes=V7X_VMEM_LIMIT_BYTES)


def _ln(x):
    mu = jnp.mean(x, axis=-1, keepdims=True)
    xc = x - mu
    var = jnp.mean(xc * xc, axis=-1, keepdims=True)
    return xc * lax.rsqrt(var + LN_EPS)


def _sigmoid(x):
    return 1.0 / (1.0 + jnp.exp(-x))


def _mods_kernel(c_ref, w_ref, b_ref, o_ref):
    c = c_ref[...]
    s = (c * _sigmoid(c)).astype(BF16)
    o_ref[0] = jnp.dot(s, w_ref[0].astype(BF16), preferred_element_type=F32) + b_ref[0]


def _mods(c8, ada_w, ada_b):
    n = 6 * D_MODEL
    return pl.pallas_call(
        _mods_kernel,
        grid=(DEPTH, n // MODS_TN),
        in_specs=[
            pl.BlockSpec((SUBLANES, D_MODEL), lambda l, j: (0, 0)),
            pl.BlockSpec((1, D_MODEL, MODS_TN), lambda l, j: (l, 0, j)),
            pl.BlockSpec((1, 1, MODS_TN), lambda l, j: (l, 0, j)),
        ],
        out_specs=pl.BlockSpec((1, SUBLANES, MODS_TN), lambda l, j: (l, 0, j)),
        out_shape=jax.ShapeDtypeStruct((DEPTH, SUBLANES, n), F32),
        compiler_params=_cparams("arbitrary", "arbitrary"),
        name="mods",
    )(c8, ada_w, ada_b.reshape(DEPTH, 1, n))


def _ln_proj_kernel(x_ref, m_ref, w_ref, o_ref, u_scr, *, col_block0):
    @pl.when(pl.program_id(1) == 0)
    def _():
        m = m_ref[0]
        y = _ln(x_ref[...])
        u_scr[...] = (y * (1.0 + m[:, D_MODEL:2 * D_MODEL]) + m[:, :D_MODEL]).astype(BF16)

    w = w_ref[col_block0 + pl.program_id(1)]
    o_ref[...] = jnp.dot(u_scr[...], w, preferred_element_type=F32).astype(o_ref.dtype)


def _ln_proj(x2d, mods3, w_tiles, col_block0, n_cols, mod_row_of_tile):
    n_rows = x2d.shape[0]
    return pl.pallas_call(
        functools.partial(_ln_proj_kernel, col_block0=col_block0),
        grid=(n_rows // PROJ_TM, n_cols // PROJ_TN),
        in_specs=[
            pl.BlockSpec((PROJ_TM, D_MODEL), lambda i, j: (i, 0)),
            pl.BlockSpec((1, 1, 6 * D_MODEL), lambda i, j: (mod_row_of_tile(i), 0, 0)),
            pl.BlockSpec(w_tiles.shape, lambda i, j: (0, 0, 0), pipeline_mode=pl.Buffered(1)),
        ],
        out_specs=pl.BlockSpec((PROJ_TM, PROJ_TN), lambda i, j: (i, j)),
        out_shape=jax.ShapeDtypeStruct((n_rows, n_cols), BF16),
        scratch_shapes=[pltpu.VMEM((PROJ_TM, D_MODEL), BF16)],
        compiler_params=_cparams("arbitrary", "arbitrary"),
        name="ln_proj",
    )(x2d, mods3, w_tiles)


def _bias_table_kernel(rpb_ref, o_ref):
    h = pl.program_id(0)
    qc = lax.broadcasted_iota(jnp.int32, (GRID_W, GRID_W), 0)
    kc = lax.broadcasted_iota(jnp.int32, (GRID_W, GRID_W), 1)
    col_start = jnp.clip(qc - WIN_W // 2, 0, GRID_W - WIN_W)
    col_ok = (kc >= col_start) & (kc < col_start + WIN_W)
    col_idx = jnp.clip(kc - qc, -(WIN_W - 1), WIN_W - 1) + WIN_W - 1
    n_rb = 2 * WIN_H - 1
    n_cb = 2 * WIN_W - 1
    tiles = []
    for ri in range(n_rb):
        acc = jnp.zeros((GRID_W, GRID_W), F32)
        for i in range(n_cb):
            acc = jnp.where(col_idx == i, rpb_ref[(h * n_rb + ri) * n_cb + i], acc)
        tiles.append(jnp.where(col_ok, acc, NEG_INF))
    masked = jnp.full((GRID_W, GRID_W), NEG_INF, F32)
    for typ, blk in enumerate((0, 1, N_QBLOCKS - 1)):
        ks = min(max(blk * ATT_QROWS - WIN_H // 2, 0), GRID_ROWS - ATT_KROWS)
        for a in range(ATT_QROWS):
            r = blk * ATT_QROWS + a
            rs = min(max(r - WIN_H // 2, 0), GRID_ROWS - WIN_H)
            strip = [tiles[ks + c - r + WIN_H - 1] if rs <= ks + c < rs + WIN_H else masked
                     for c in range(ATT_KROWS)]
            o_ref[0, typ, a * GRID_W:(a + 1) * GRID_W, :] = jnp.concatenate(strip, axis=1)


def _bias_table(rpb_l):
    flat = rpb_l.reshape(-1).astype(F32)
    return pl.pallas_call(
        _bias_table_kernel,
        grid=(N_HEADS,),
        in_specs=[pl.BlockSpec(memory_space=pltpu.SMEM)],
        out_specs=pl.BlockSpec((1, 3, ATT_Q, ATT_K), lambda h: (h, 0, 0, 0)),
        out_shape=jax.ShapeDtypeStruct((N_HEADS, 3, ATT_Q, ATT_K), F32),
        compiler_params=_cparams("arbitrary"),
        name="bias_table",
    )(flat)


def _natten_kernel(q_ref, k_ref, v_ref, kc_ref, vc_ref, bias_ref, o_ref):
    scale = HEAD_DIM ** -0.5
    kc = kc_ref[...]
    vc = vc_ref[...]
    nt = (((1,), (1,)), ((), ()))

    def body(j, carry):
        typ = jnp.where(j == 0, 0, jnp.where(j == N_QBLOCKS - 1, 2, 1))
        q0 = pl.multiple_of(j * ATT_Q, ATT_Q)
        k0 = pl.multiple_of(
            jnp.clip(j * ATT_QROWS - WIN_H // 2, 0, GRID_ROWS - ATT_KROWS) * GRID_W, 4 * GRID_W)
        q = q_ref[pl.ds(q0, ATT_Q), :]
        kw = k_ref[pl.ds(k0, ATT_K), :]
        vw = v_ref[pl.ds(k0, ATT_K), :]
        s_loc = lax.dot_general(q, kw, nt, preferred_element_type=F32) * scale + bias_ref[typ]
        s_ctx = lax.dot_general(q, kc, nt, preferred_element_type=F32) * scale
        m = jnp.maximum(jnp.max(s_loc, axis=-1, keepdims=True), jnp.max(s_ctx, axis=-1, keepdims=True))
        p_loc = jnp.exp(s_loc - m)
        p_ctx = jnp.exp(s_ctx - m)
        denom = jnp.sum(p_loc, axis=-1, keepdims=True) + jnp.sum(p_ctx, axis=-1, keepdims=True)
        o = (jnp.dot(p_loc.astype(BF16), vw, preferred_element_type=F32)
             + jnp.dot(p_ctx.astype(BF16), vc, preferred_element_type=F32))
        o_ref[pl.ds(q0, ATT_Q), :] = (o / denom).astype(o_ref.dtype)
        return carry

    lax.fori_loop(0, N_QBLOCKS, body, 0)


def _natten(proj, kvc, bias):
    blk = lambda off: pl.BlockSpec((SEQ, HEAD_DIM), lambda b, h: (b, off + h))
    cblk = lambda off: pl.BlockSpec((CTX_LEN, HEAD_DIM), lambda b, h: (b, off + h))
    return pl.pallas_call(
        _natten_kernel,
        grid=(BATCH, N_HEADS),
        in_specs=[blk(0), blk(N_HEADS), blk(2 * N_HEADS), cblk(0), cblk(N_HEADS),
                  pl.BlockSpec((None, 3, ATT_Q, ATT_K), lambda b, h: (h, 0, 0, 0))],
        out_specs=pl.BlockSpec((SEQ, HEAD_DIM), lambda b, h: (b, h)),
        out_shape=jax.ShapeDtypeStruct((N_TOK, ATTN_WIDTH), BF16),
        compiler_params=_cparams("arbitrary", "arbitrary"),
        name="natten",
    )(proj, proj, proj, kvc, kvc, bias)


def _pack_bf16_pair(lo, hi):
    return pltpu.pack_elementwise([lo, hi], packed_dtype=BF16)


def _unpack_bf16_pair(w):
    unpack = lambda i: pltpu.unpack_elementwise(w, index=i, packed_dtype=BF16, unpacked_dtype=F32)
    return unpack(0), unpack(1)


def _store_token_tiles(ref, vals):
    tm, n = vals.shape[0], vals.shape[1] // LANES
    for c in range(n):
        ref[pl.ds(c, tm, stride=n), :] = vals[:, c * LANES:(c + 1) * LANES]


def _load_token_tiles(ref, tm):
    n = ref.shape[0] // tm
    return jnp.concatenate([ref[pl.ds(c, tm, stride=n), :] for c in range(n)], axis=1)


def _epilogue(y, s, x_ref, m_ref, lng_ref, lnb_ref, rwh_ref, rwl_ref, rb_ref,
              x1_ref, u2_ref, re_ref, rw_ref, rk_ref, cnt_ref):
    tm = MIX_TM
    rows = slice(s * tm, (s + 1) * tm)
    m = m_ref[0]
    g1 = m[:, 2 * D_MODEL:3 * D_MODEL]
    sh2 = m[:, 3 * D_MODEL:4 * D_MODEL]
    sc2 = m[:, 4 * D_MODEL:5 * D_MODEL]
    x1 = _ln(ALPHA * x_ref[rows, :] + g1 * y) * lng_ref[0:1, :] + lnb_ref[0:1, :]
    x1_ref[rows, :] = x1
    u2 = _ln(x1) * (1.0 + sc2) + sh2
    _store_token_tiles(u2_ref.at[pl.ds(s * tm * ROW_CHUNKS, tm * ROW_CHUNKS)],
                       _pack_bf16_pair(u2[:, :D_MODEL // 2], u2[:, D_MODEL // 2:]))

    u_hi = u2.astype(BF16)
    u_lo = (u2 - u_hi.astype(F32)).astype(BF16)
    logits = (jnp.dot(u_hi, rwh_ref[...], preferred_element_type=F32)
              + jnp.dot(u_lo, rwh_ref[...], preferred_element_type=F32)
              + jnp.dot(u_hi, rwl_ref[...], preferred_element_type=F32))
    aff = _sigmoid(logits.T[:N_EXPERTS, :])
    biased = aff + rb_ref[...]
    a = [biased[l * N_GROUPS:(l + 1) * N_GROUPS, :] for l in range(EXPERTS_PER_GROUP)]
    f = [aff[l * N_GROUPS:(l + 1) * N_GROUPS, :] for l in range(EXPERTS_PER_GROUP)]
    hi01, lo01 = jnp.maximum(a[0], a[1]), jnp.minimum(a[0], a[1])
    hi23, lo23 = jnp.maximum(a[2], a[3]), jnp.minimum(a[2], a[3])
    top1 = jnp.maximum(hi01, hi23)
    top2 = jnp.maximum(jnp.minimum(hi01, hi23), jnp.maximum(lo01, lo23))
    gscore = top1 + top2
    gi = lax.broadcasted_iota(jnp.int32, (N_GROUPS, tm), 0)
    gmax = jnp.max(gscore, axis=0, keepdims=True)
    group = jnp.min(jnp.where(gscore == gmax, gi, N_GROUPS), axis=0, keepdims=True)
    sel = gi == group
    b = [jnp.sum(jnp.where(sel, a[l], 0.0), axis=0, keepdims=True) for l in range(EXPERTS_PER_GROUP)]
    c = [jnp.sum(jnp.where(sel, f[l], 0.0), axis=0, keepdims=True) for l in range(EXPERTS_PER_GROUP)]

    def first_argmax(vals):
        mx = functools.reduce(jnp.maximum, vals)
        idx = jnp.full(mx.shape, len(vals) - 1, jnp.int32)
        for l in range(len(vals) - 2, -1, -1):
            idx = jnp.where(vals[l] == mx, l, idx)
        return idx

    i1 = first_argmax(b)
    i2 = first_argmax([jnp.where(i1 == l, -jnp.inf, b[l]) for l in range(EXPERTS_PER_GROUP)])
    pick = lambda idx: functools.reduce(
        lambda acc, l: jnp.where(idx == l, c[l], acc), range(1, EXPERTS_PER_GROUP), c[0])
    w1, w2 = pick(i1), pick(i2)
    wsum = w1 + w2
    e1 = group * EXPERTS_PER_GROUP + i1
    e2 = group * EXPERTS_PER_GROUP + i2
    re_ref[0:1, rows] = e1
    re_ref[1:2, rows] = e2
    rw_ref[0:1, rows] = w1 / wsum
    rw_ref[1:2, rows] = w2 / wsum

    ei = lax.broadcasted_iota(jnp.int32, (N_EXPERTS, tm), 0)
    oh1 = ei == e1
    oh2 = ei == e2
    oh = jnp.where(oh1 | oh2, 1.0, 0.0)
    t_row = lax.broadcasted_iota(jnp.int32, (tm, tm), 0)
    t_col = lax.broadcasted_iota(jnp.int32, (tm, tm), 1)
    before = jnp.where(t_row < t_col, 1.0, 0.0).astype(BF16)
    prefix = jnp.dot(oh.astype(BF16), before, preferred_element_type=F32)
    rk_ref[0:1, rows] = jnp.sum(jnp.where(oh1, prefix, 0.0), axis=0, keepdims=True).astype(jnp.int32)
    rk_ref[1:2, rows] = jnp.sum(jnp.where(oh2, prefix, 0.0), axis=0, keepdims=True).astype(jnp.int32)
    cnt = jnp.sum(oh, axis=1, keepdims=True)
    cnt_ref[s] = jnp.broadcast_to(cnt, (N_EXPERTS, LANES)).astype(jnp.int32)


def _resident(shape):
    return pl.BlockSpec(shape, lambda i: (0,) * len(shape), pipeline_mode=pl.Buffered(1))


def _epilogue_specs():
    blocks_per_seq = SEQ // MIX_BLOCK
    n_tiles = N_TOK // MIX_TM
    in_specs = [
        pl.BlockSpec((MIX_BLOCK, D_MODEL), lambda i: (i, 0)),
        pl.BlockSpec((1, 1, 6 * D_MODEL), lambda i: (i // blocks_per_seq, 0, 0)),
        _resident((2, D_MODEL)),
        _resident((2, D_MODEL)),
        _resident((D_MODEL, LANES)),
        _resident((D_MODEL, LANES)),
        _resident((N_EXPERTS, 1)),
    ]
    out_specs = [
        pl.BlockSpec((MIX_BLOCK, D_MODEL), lambda i: (i, 0)),
        pl.BlockSpec((MIX_BLOCK * ROW_CHUNKS, LANES), lambda i: (i, 0)),
        pl.BlockSpec((TOP_K, MIX_BLOCK), lambda i: (0, i)),
        pl.BlockSpec((TOP_K, MIX_BLOCK), lambda i: (0, i)),
        pl.BlockSpec((TOP_K, MIX_BLOCK), lambda i: (0, i)),
        pl.BlockSpec((MIX_SUBTILES, N_EXPERTS, LANES), lambda i: (i, 0, 0)),
    ]
    out_shape = [
        jax.ShapeDtypeStruct((N_TOK, D_MODEL), F32),
        jax.ShapeDtypeStruct((N_TOK * ROW_CHUNKS, LANES), jnp.uint32),
        jax.ShapeDtypeStruct((TOP_K, N_TOK), jnp.int32),
        jax.ShapeDtypeStruct((TOP_K, N_TOK), F32),
        jax.ShapeDtypeStruct((TOP_K, N_TOK), jnp.int32),
        jax.ShapeDtypeStruct((n_tiles, N_EXPERTS, LANES), jnp.int32),
    ]
    return in_specs, out_specs, out_shape


def _mix_out_kernel(attn_ref, gb_ref, gc_ref, xin_ref, gcp_ref, xinp_ref, gcn_ref, xinn_ref,
                    cw_ref, wo_ref, *epilogue_refs):
    i = pl.program_id(0)
    blocks_per_seq = SEQ // MIX_BLOCK
    z = gc_ref[...].astype(F32) * xin_ref[...].astype(F32)
    halo = 2 * SUBLANES
    zp_row = (gcp_ref[...].astype(F32) * xinp_ref[...].astype(F32))[halo - 1:halo, :]
    zn_row = (gcn_ref[...].astype(F32) * xinn_ref[...].astype(F32))[0:1, :]
    zp_row = jnp.where(i % blocks_per_seq == 0, 0.0, zp_row)
    zn_row = jnp.where(i % blocks_per_seq == blocks_per_seq - 1, 0.0, zn_row)
    row = lax.broadcasted_iota(jnp.int32, (MIX_BLOCK, 1), 0)
    z_prev = jnp.where(row == 0, zp_row, pltpu.roll(z, 1, axis=0))
    z_next = jnp.where(row == MIX_BLOCK - 1, zn_row, pltpu.roll(z, MIX_BLOCK - 1, axis=0))
    conv = cw_ref[0:1, :] * z_prev + cw_ref[1:2, :] * z + cw_ref[2:3, :] * z_next
    gated = (gb_ref[...].astype(F32) * conv).astype(BF16)
    y = (jnp.dot(attn_ref[...], wo_ref[:ATTN_WIDTH, :], preferred_element_type=F32)
         + jnp.dot(gated, wo_ref[ATTN_WIDTH:, :], preferred_element_type=F32))
    for s in range(MIX_SUBTILES):
        _epilogue(y[s * MIX_TM:(s + 1) * MIX_TM, :], s, *epilogue_refs)


def _mix_out(attn, proj, conv_w, w_out, x2d, mods3, ln_g, ln_b, rw_hi, rw_lo, rb):
    halo = 2 * SUBLANES
    hb = MIX_BLOCK // halo
    n_halo_blocks = N_TOK // halo
    cblk = lambda off: pl.BlockSpec((MIX_BLOCK, CONV_CH), lambda i: (i, off))
    prev = lambda off: pl.BlockSpec((halo, CONV_CH), lambda i: (jnp.maximum(i * hb - 1, 0), off))
    nxt = lambda off: pl.BlockSpec(
        (halo, CONV_CH), lambda i: (jnp.minimum((i + 1) * hb, n_halo_blocks - 1), off))
    ep_in, out_specs, out_shape = _epilogue_specs()
    return pl.pallas_call(
        _mix_out_kernel,
        grid=(N_TOK // MIX_BLOCK,),
        in_specs=[pl.BlockSpec((MIX_BLOCK, ATTN_WIDTH), lambda i: (i, 0)),
                  cblk(3), cblk(4), cblk(5), prev(4), prev(5), nxt(4), nxt(5),
                  _resident((3, CONV_CH)), _resident((D_MODEL, D_MODEL))] + ep_in,
        out_specs=out_specs,
        out_shape=out_shape,
        compiler_params=_cparams("arbitrary"),
        name="mix_out",
    )(attn, proj, proj, proj, proj, proj, proj, proj, conv_w, w_out, x2d, mods3, ln_g, ln_b,
      rw_hi, rw_lo, rb)


def _dft_seq_kernel(cs_ref, tw_ref, u0_ref, u1_ref, u2_ref, u3_ref, o_ref):
    c_m, s_m = cs_ref[0], cs_ref[1]
    tr, ti = [], []
    for n2, u_ref in enumerate((u0_ref, u1_ref, u2_ref, u3_ref)):
        x = u_ref[...]
        a = jnp.dot(c_m, x, preferred_element_type=F32)
        b = jnp.dot(s_m, x, preferred_element_type=F32)
        if n2 == 0:
            tr.append(a)
            ti.append(b)
        else:
            c = tw_ref[:, 2 * n2 - 2:2 * n2 - 1]
            s = tw_ref[:, 2 * n2 - 1:2 * n2]
            tr.append(a * c - b * s)
            ti.append(a * s + b * c)
    o_ref[0, 0] = (tr[0] + tr[1] + tr[2] + tr[3]).astype(o_ref.dtype)
    o_ref[1, 0] = (ti[0] + ti[1] + ti[2] + ti[3]).astype(o_ref.dtype)
    o_ref[0, 1] = (tr[0] - ti[1] - tr[2] + ti[3]).astype(o_ref.dtype)
    o_ref[1, 1] = (ti[0] + tr[1] - ti[2] - tr[3]).astype(o_ref.dtype)
    o_ref[0, 2] = (tr[0] - tr[1] + tr[2] - tr[3]).astype(o_ref.dtype)
    o_ref[1, 2] = (ti[0] - ti[1] + ti[2] - ti[3]).astype(o_ref.dtype)
    o_ref[0, 3] = (tr[0] + ti[1] - tr[2] - ti[3]).astype(o_ref.dtype)
    o_ref[1, 3] = (ti[0] - tr[1] - ti[2] + tr[3]).astype(o_ref.dtype)


def _dft_seq(cs, tw, u4):
    n_col = D_MODEL // DFT_TN
    sub = lambda n2: pl.BlockSpec((DFT_SUB, DFT_TN), lambda b, n, i: (b, n2 * n_col + n))
    out = pl.pallas_call(
        _dft_seq_kernel,
        grid=(BATCH, n_col, DFT_SUB // DFT_TK),
        in_specs=[pl.BlockSpec((2, DFT_TK, DFT_SUB), lambda b, n, i: (0, i, 0)),
                  pl.BlockSpec((DFT_TK, 2 * DFT_RADIX), lambda b, n, i: (i, 0)),
                  sub(0), sub(1), sub(2), sub(3)],
        out_specs=pl.BlockSpec((None, 2, DFT_RADIX, DFT_TK, DFT_TN), lambda b, n, i: (b, 0, 0, i, n)),
        out_shape=jax.ShapeDtypeStruct((BATCH, 2, DFT_RADIX, DFT_SUB, D_MODEL), BF16),
        compiler_params=_cparams("arbitrary", "arbitrary", "arbitrary"),
        name="dft_seq",
    )(cs, tw, u4, u4, u4, u4)
    return out.reshape(BATCH * 2 * SEQ, D_MODEL)


def _four_out_kernel(p_ref, q_ref, cc_ref, sc_ref, wf_ref, *epilogue_refs):
    y = jnp.zeros((MIX_BLOCK, D_MODEL), F32)
    for g in range(N_FOURIER_GROUPS):
        cols = slice(g * FOURIER_CH, (g + 1) * FOURIER_CH)
        fg = (jnp.dot(p_ref[:, cols], cc_ref[...], preferred_element_type=F32)
              - jnp.dot(q_ref[:, cols], sc_ref[...], preferred_element_type=F32))
        y = y + jnp.dot(fg.astype(BF16), wf_ref[cols, :], preferred_element_type=F32)
    for s in range(MIX_SUBTILES):
        _epilogue(y[s * MIX_TM:(s + 1) * MIX_TM, :], s, *epilogue_refs)


def _four_out(pq, cc, sc, wf, x2d, mods3, ln_g, ln_b, rw_hi, rw_lo, rb):
    blocks_per_seq = SEQ // MIX_BLOCK
    ep_in, out_specs, out_shape = _epilogue_specs()
    prow = lambda i: (i // blocks_per_seq) * 2 * blocks_per_seq + i % blocks_per_seq
    return pl.pallas_call(
        _four_out_kernel,
        grid=(N_TOK // MIX_BLOCK,),
        in_specs=[pl.BlockSpec((MIX_BLOCK, D_MODEL), lambda i: (prow(i), 0)),
                  pl.BlockSpec((MIX_BLOCK, D_MODEL), lambda i: (prow(i) + blocks_per_seq, 0)),
                  _resident((FOURIER_CH, FOURIER_CH)), _resident((FOURIER_CH, FOURIER_CH)),
                  _resident((D_MODEL, D_MODEL))] + ep_in,
        out_specs=out_specs,
        out_shape=out_shape,
        compiler_params=_cparams("arbitrary"),
        name="four_out",
    )(pq, pq, cc, sc, wf, x2d, mods3, ln_g, ln_b, rw_hi, rw_lo, rb)


def _dispatch_kernel(dest, fill_start, fill_n, u_ref, out_hbm, sem):
    i = pl.program_id(0)
    base = i * DISPATCH_TM

    def row_dma(r, d):
        src = u_ref.at[pl.ds(pl.multiple_of(r * ROW_CHUNKS, ROW_CHUNKS), ROW_CHUNKS)]
        dst = out_hbm.at[pl.ds(pl.multiple_of(d * ROW_CHUNKS, ROW_CHUNKS), ROW_CHUNKS)]
        return pltpu.make_async_copy(src, dst, sem)

    def issue(r, c):
        for k in range(TOP_K):
            row_dma(r, dest[k * N_TOK + base + r]).start(priority=k)
        return c

    def drain(r, c):
        for k in range(TOP_K):
            row_dma(0, 0).wait()
        return c

    lax.fori_loop(0, DISPATCH_TM, issue, 0, unroll=8)
    lax.fori_loop(0, DISPATCH_TM, drain, 0, unroll=8)

    @pl.when(i == pl.num_programs(0) - 1)
    def _():
        def per_segment(e, c):
            lax.fori_loop(0, fill_n[e], lambda r, c2: (row_dma(0, fill_start[e] + r).start(), c2)[1], 0)
            lax.fori_loop(0, fill_n[e], lambda r, c2: (row_dma(0, 0).wait(), c2)[1], 0)
            return c

        lax.fori_loop(0, N_FILL_SEGMENTS, per_segment, 0)


def _dispatch(u2p, dest, fill_start, fill_n):
    return pl.pallas_call(
        _dispatch_kernel,
        grid_spec=pltpu.PrefetchScalarGridSpec(
            num_scalar_prefetch=3,
            grid=(N_TOK // DISPATCH_TM,),
            in_specs=[pl.BlockSpec((DISPATCH_TM * ROW_CHUNKS, LANES), lambda i, *_: (i, 0))],
            out_specs=pl.BlockSpec(memory_space=pl.ANY),
            scratch_shapes=[pltpu.SemaphoreType.DMA(())],
        ),
        out_shape=jax.ShapeDtypeStruct((N_PAD * ROW_CHUNKS, LANES), jnp.uint32),
        compiler_params=_cparams("arbitrary"),
        name="dispatch",
    )(dest, fill_start, fill_n, u2p)


def _expert_weights(layer, be_ref, nxt_ref, w_hbm, stage, sem, w_bf16):
    i = pl.program_id(0)
    changed = (i == 0) | (be_ref[i] != be_ref[jnp.maximum(i - 1, 0)])

    def copies(e):
        out = []
        for w, st in zip(w_hbm, stage):
            rows = st.shape[0] // WEIGHT_DMA_CHUNKS
            for c in range(WEIGHT_DMA_CHUNKS):
                rs = pl.ds(c * rows, rows)
                out.append(pltpu.make_async_copy(w.at[layer, e, rs], st.at[rs], sem))
        return out

    @pl.when(changed)
    def _():
        @pl.when(i == 0)
        def _():
            for cp in copies(be_ref[0]):
                cp.start(priority=1)

        for cp in copies(be_ref[i]):
            cp.wait()
        for st, wb in zip(stage, w_bf16):
            def cast_rows(r, c, st=st, wb=wb):
                rs = pl.ds(pl.multiple_of(r * CAST_ROWS, CAST_ROWS), CAST_ROWS)
                wb[rs, :] = st[rs, :].astype(BF16)
                return c
            lax.fori_loop(0, st.shape[0] // CAST_ROWS, cast_rows, 0)

        @pl.when(nxt_ref[i] >= 0)
        def _():
            for cp in copies(nxt_ref[i]):
                cp.start(priority=1)


def _experts_kernel(be_ref, nv_ref, nxt_ref, x_ref, wg_hbm, wu_hbm, wd_hbm, o_ref,
                    wg_st, wu_st, wd_st, sem, wg_b, wu_b, wd_b, *, layer):
    _expert_weights(layer, be_ref, nxt_ref, (wg_hbm, wu_hbm, wd_hbm), (wg_st, wu_st, wd_st), sem,
                    (wg_b, wu_b, wd_b))

    @pl.when(pl.program_id(0) >= nv_ref[0])
    def _():
        o_ref[...] = jnp.zeros_like(o_ref)

    @pl.when(pl.program_id(0) < nv_ref[0])
    def _():
        lo, hi = _unpack_bf16_pair(_load_token_tiles(x_ref, EXPERT_TM))
        lo, hi = lo.astype(BF16), hi.astype(BF16)
        half = D_MODEL // 2
        gate = (jnp.dot(lo, wg_b[:half, :], preferred_element_type=F32)
                + jnp.dot(hi, wg_b[half:, :], preferred_element_type=F32))
        up = (jnp.dot(lo, wu_b[:half, :], preferred_element_type=F32)
              + jnp.dot(hi, wu_b[half:, :], preferred_element_type=F32))
        act = (gate * _sigmoid(gate) * up).astype(BF16)
        _store_token_tiles(o_ref, jnp.dot(act, wd_b[...], preferred_element_type=F32))


def _experts(layer, sched, xs, w_gate, w_up, w_down):
    any_spec = pl.BlockSpec(memory_space=pl.ANY)
    up_shape, down_shape = (D_MODEL, D_EXPERT), (D_EXPERT, D_MODEL)
    return pl.pallas_call(
        functools.partial(_experts_kernel, layer=layer),
        grid_spec=pltpu.PrefetchScalarGridSpec(
            num_scalar_prefetch=3,
            grid=(N_EXPERT_BLOCKS,),
            in_specs=[pl.BlockSpec((EXPERT_TM * ROW_CHUNKS, LANES), lambda i, *_: (i, 0)),
                      any_spec, any_spec, any_spec],
            out_specs=pl.BlockSpec((EXPERT_TM * OUT_CHUNKS, LANES), lambda i, *_: (i, 0)),
            scratch_shapes=[pltpu.VMEM(up_shape, F32), pltpu.VMEM(up_shape, F32),
                            pltpu.VMEM(down_shape, F32), pltpu.SemaphoreType.DMA(()),
                            pltpu.VMEM(up_shape, BF16), pltpu.VMEM(up_shape, BF16),
                            pltpu.VMEM(down_shape, BF16)],
        ),
        out_shape=jax.ShapeDtypeStruct((N_PAD * OUT_CHUNKS, LANES), F32),
        compiler_params=_cparams("arbitrary"),
        name="experts",
    )(*sched, xs, w_gate, w_up, w_down)


def _final_kernel(dest, ys_hbm, rw_ref, x1_ref, m_ref, lng_ref, lnb_ref, *rest, emit_next):
    if emit_next:
        mn_ref, x2_ref, un_ref, buf, sem, un_scr = rest
    else:
        x2_ref, buf, sem = rest
    i = pl.program_id(0)
    n = pl.num_programs(0)

    def row_dma(tile, slot, r, k):
        d = dest[k * N_TOK + tile * MIX_TM + r]
        src = ys_hbm.at[pl.ds(pl.multiple_of(d * OUT_CHUNKS, OUT_CHUNKS), OUT_CHUNKS)]
        dst = buf.at[slot, k, pl.ds(pl.multiple_of(r * OUT_CHUNKS, OUT_CHUNKS), OUT_CHUNKS)]
        return pltpu.make_async_copy(src, dst, sem.at[slot])

    def gather(tile, slot):
        def issue(r, c):
            for k in range(TOP_K):
                row_dma(tile, slot, r, k).start(priority=k)
            return c
        lax.fori_loop(0, MIX_TM, issue, 0, unroll=8)

    @pl.when(i == 0)
    def _():
        gather(0, 0)

    @pl.when(i + 1 < n)
    def _():
        gather(i + 1, (i + 1) % 2)

    slot = i % 2

    def drain(r, c):
        for k in range(TOP_K):
            row_dma(0, slot, 0, k).wait()
        return c
    lax.fori_loop(0, MIX_TM, drain, 0, unroll=8)

    expert_rows = lambda k: _load_token_tiles(buf.at[slot, k], MIX_TM)
    m = m_ref[0]
    g2 = m[:, 5 * D_MODEL:6 * D_MODEL]
    f = rw_ref[:, 0:1] * expert_rows(0) + rw_ref[:, 1:2] * expert_rows(1)
    x2 = _ln(ALPHA * x1_ref[...] + g2 * f) * lng_ref[1:2, :] + lnb_ref[1:2, :]
    x2_ref[...] = x2
    if emit_next:
        mn = mn_ref[0]
        un = _ln(x2) * (1.0 + mn[:, D_MODEL:2 * D_MODEL]) + mn[:, :D_MODEL]
        for c in range(OUT_CHUNKS):
            un_scr[c] = un[:, c * LANES:(c + 1) * LANES]
        for n2 in range(DFT_RADIX):
            rows = pl.ds(n2, MIX_TM // DFT_RADIX, stride=DFT_RADIX)
            un_ref[:, n2 * D_MODEL:(n2 + 1) * D_MODEL] = jnp.concatenate(
                [un_scr[c, rows, :] for c in range(OUT_CHUNKS)], axis=1).astype(BF16)


def _final(dest, ys, rw_t, x1, mods3, ln_g, ln_b, mods3_next=None):
    tiles_per_seq = SEQ // MIX_TM
    emit_next = mods3_next is not None
    mspec = pl.BlockSpec((1, 1, 6 * D_MODEL), lambda i, d: (i // tiles_per_seq, 0, 0))
    row = pl.BlockSpec((MIX_TM, D_MODEL), lambda i, d: (i, 0))
    in_specs = [pl.BlockSpec(memory_space=pl.ANY),
                pl.BlockSpec((MIX_TM, TOP_K), lambda i, d: (i, 0)),
                row, mspec,
                pl.BlockSpec((2, D_MODEL), lambda i, d: (0, 0)),
                pl.BlockSpec((2, D_MODEL), lambda i, d: (0, 0))]
    args = [ys, rw_t, x1, mods3, ln_g, ln_b]
    out_specs = [row]
    out_shape = [jax.ShapeDtypeStruct((N_TOK, D_MODEL), F32)]
    scratch = [pltpu.VMEM((2, TOP_K, MIX_TM * OUT_CHUNKS, LANES), F32), pltpu.SemaphoreType.DMA((2,))]
    if emit_next:
        in_specs.append(mspec)
        args.append(mods3_next)
        out_specs.append(pl.BlockSpec((MIX_TM // DFT_RADIX, DFT_RADIX * D_MODEL), lambda i, d: (i, 0)))
        out_shape.append(jax.ShapeDtypeStruct((N_TOK // DFT_RADIX, DFT_RADIX * D_MODEL), BF16))
        scratch.append(pltpu.VMEM((OUT_CHUNKS, MIX_TM, LANES), F32))
    return pl.pallas_call(
        functools.partial(_final_kernel, emit_next=emit_next),
        grid_spec=pltpu.PrefetchScalarGridSpec(
            num_scalar_prefetch=1,
            grid=(N_TOK // MIX_TM,),
            in_specs=in_specs,
            out_specs=out_specs,
            scratch_shapes=scratch,
        ),
        out_shape=out_shape,
        compiler_params=_cparams("arbitrary"),
        name="final",
    )(dest, *args)


def _moe(layer, u2p, route_e, route_w, rank, cnt, w_gate, w_up, w_down):
    n_tiles = N_TOK // MIX_TM
    cnt = cnt[:, :, 0]
    counts = jnp.sum(cnt, axis=0)
    padded = (counts + EXPERT_TM - 1) // EXPERT_TM * EXPERT_TM
    pad_end = jnp.cumsum(padded)
    pad_start = pad_end - padded
    base = pad_start[None, :] + jnp.cumsum(cnt, axis=0) - cnt
    e_r = route_e.reshape(TOP_K, n_tiles, MIX_TM)
    onehot = e_r[..., None] == jnp.arange(N_EXPERTS, dtype=jnp.int32)
    dest = jnp.sum(jnp.where(onehot, base[None, :, None, :], 0), axis=-1).reshape(TOP_K, N_TOK) + rank
    dest = dest.reshape(-1).astype(jnp.int32)
    n_valid = (pad_end[-1] // EXPERT_TM).astype(jnp.int32).reshape(1)
    block_row0 = jnp.arange(N_EXPERT_BLOCKS, dtype=jnp.int32) * EXPERT_TM
    present = padded > 0
    expert_ids = jnp.arange(N_EXPERTS, dtype=jnp.int32)
    last_expert = jnp.max(jnp.where(present, expert_ids, 0))
    block_expert = jnp.minimum(
        jnp.sum(pad_end[None, :] <= block_row0[:, None], axis=1), last_expert).astype(jnp.int32)
    later = present[None, :] & (expert_ids[None, :] > block_expert[:, None])
    block_next = jnp.min(jnp.where(later, expert_ids[None, :], N_EXPERTS), axis=1)
    block_next = jnp.where(block_next == N_EXPERTS, -1, block_next)
    sched = (block_expert, n_valid, block_next.astype(jnp.int32))

    fill_start = jnp.concatenate([pad_start + counts, pad_end[-1:]]).astype(jnp.int32)
    fill_n = jnp.concatenate([padded - counts, N_PAD - pad_end[-1:]]).astype(jnp.int32)
    xs = _dispatch(u2p, dest, fill_start, fill_n)
    ys = _experts(layer, sched, xs, w_gate, w_up, w_down)
    return dest, ys, route_w.T


def _dft_tables():
    dft = lambda n: 2 * np.pi * (np.outer(np.arange(n), np.arange(n)) % n) / n
    ang = dft(DFT_SUB)
    cs = jnp.asarray(np.stack([np.cos(ang), np.sin(ang)]) * SEQ ** -0.5, F32).astype(BF16)
    k1 = np.arange(DFT_SUB, dtype=np.float64)
    tw = np.zeros((DFT_SUB, 2 * DFT_RADIX), np.float64)
    for n2 in range(1, DFT_RADIX):
        tw[:, 2 * n2 - 2] = np.cos(2 * np.pi * n2 * k1 / SEQ)
        tw[:, 2 * n2 - 1] = np.sin(2 * np.pi * n2 * k1 / SEQ)
    ang = dft(FOURIER_CH)
    cc = jnp.asarray(np.cos(ang) * FOURIER_CH ** -0.5, F32).astype(BF16)
    sc = jnp.asarray(np.sin(ang) * FOURIER_CH ** -0.5, F32).astype(BF16)
    return cs, jnp.asarray(tw, F32), cc, sc


def kernel(x, c, ctx, c_ctx, ada_w, ada_b, w_mix_in, rpb, conv_w, w_mix_out, w_fourier_out,
           router_w, router_b, w_gate, w_up, w_down, ln_g, ln_b):
    x2d = x.reshape(N_TOK, D_MODEL)
    c8 = jnp.concatenate(
        [c, c_ctx[None, :], jnp.zeros((SUBLANES - BATCH - 1, D_MODEL), F32)], axis=0)
    mods = _mods(c8, ada_w, ada_b)
    mods3 = [mods[l].reshape(SUBLANES, 1, 6 * D_MODEL) for l in range(DEPTH)]
    perm = lambda a: a.reshape(N_GROUPS, EXPERTS_PER_GROUP, -1).transpose(1, 0, 2).reshape(N_EXPERTS, -1)
    rw = jnp.pad(perm(router_w.T).T, ((0, 0), (0, LANES - N_EXPERTS)))
    rw_hi = rw.astype(BF16)
    rw_lo = (rw - rw_hi.astype(F32)).astype(BF16)
    rb = perm(router_b.reshape(N_EXPERTS, 1))
    tiles_per_seq = SEQ // PROJ_TM

    w_in = (w_mix_in[0].astype(BF16)
            .reshape(D_MODEL, MIX_IN_WIDTH // PROJ_TN, PROJ_TN).transpose(1, 0, 2))
    proj = _ln_proj(x2d, mods3[0], w_in, 0, MIX_IN_WIDTH, lambda i: i // tiles_per_seq)
    kvc = _ln_proj(ctx.reshape(BATCH * CTX_LEN, D_MODEL), mods3[0], w_in,
                   ATTN_WIDTH // PROJ_TN, 2 * ATTN_WIDTH, lambda i: BATCH)
    attn = _natten(proj, kvc, _bias_table(rpb[0]))
    x1, u2p, route_e, route_w, rank, cnt = _mix_out(
        attn, proj, conv_w[0], w_mix_out[0].astype(BF16), x2d, mods3[0], ln_g[0], ln_b[0],
        rw_hi, rw_lo, rb)
    dest, ys, rw_t = _moe(0, u2p, route_e, route_w, rank, cnt, w_gate, w_up, w_down)
    x2, u_next = _final(dest, ys, rw_t, x1, mods3[0], ln_g[0], ln_b[0], mods3_next=mods3[1])

    cs, tw, cc, sc = _dft_tables()
    pq = _dft_seq(cs, tw, u_next)
    x1, u2p, route_e, route_w, rank, cnt = _four_out(
        pq, cc, sc, w_fourier_out[0].astype(BF16), x2, mods3[1], ln_g[1], ln_b[1],
        rw_hi, rw_lo, rb)
    dest, ys, rw_t = _moe(1, u2p, route_e, route_w, rank, cnt, w_gate, w_up, w_down)
    (x3,) = _final(dest, ys, rw_t, x1, mods3[1], ln_g[1], ln_b[1])
    return x3.reshape(BATCH, SEQ, D_MODEL)
```

```python
import functools

import numpy as np
import jax
import jax.numpy as jnp
from jax import lax
from jax.experimental import pallas as pl
from jax.experimental.pallas import tpu as pltpu

D_MODEL = 2048
BATCH = 4
SEQ = 4096
DEPTH = 2
GRID_W = 64
GRID_ROWS = SEQ // GRID_W
CTX_LEN = 256
HEAD_DIM = 128
ATTN_WIDTH = D_MODEL // 2
N_HEADS = ATTN_WIDTH // HEAD_DIM
WIN_H = 8
WIN_W = 16
CONV_CH = D_MODEL // 2
MIX_IN_WIDTH = 3 * ATTN_WIDTH + 3 * CONV_CH
N_FOURIER_GROUPS = 4
FOURIER_CH = D_MODEL // N_FOURIER_GROUPS
N_EXPERTS = 32
N_GROUPS = 8
EXPERTS_PER_GROUP = N_EXPERTS // N_GROUPS
TOP_K = 2
D_EXPERT = D_MODEL // 2
ALPHA = (2 * DEPTH) ** 0.25
LN_EPS = 1e-6
NEG_INF = -1e30
N_TOK = BATCH * SEQ

V7X_VMEM_LIMIT_BYTES = 56 * 1024 * 1024
SUBLANES = 8
LANES = 128

PROJ_TM = 512
PROJ_TN = 1024
ATT_QROWS = 4
ATT_KROWS = 12
ATT_Q = ATT_QROWS * GRID_W
ATT_K = ATT_KROWS * GRID_W
N_QBLOCKS = GRID_ROWS // ATT_QROWS
MIX_TM = 256
DFT_RADIX = 4
DFT_SUB = SEQ // DFT_RADIX
DFT_TK = 256
DFT_TN = 1024
EXPERT_TM = 256
N_ROWS = N_TOK * TOP_K
N_EXPERT_BLOCKS = N_ROWS // EXPERT_TM + N_EXPERTS
N_PAD = N_EXPERT_BLOCKS * EXPERT_TM
DISPATCH_TM = 1024
WEIGHT_DMA_CHUNKS = 4
CAST_ROWS = 64
N_FILL_SEGMENTS = N_EXPERTS + 1
ROW_CHUNKS = D_MODEL // 2 // LANES
OUT_CHUNKS = D_MODEL // LANES
MIX_BLOCK = 512
MIX_SUBTILES = MIX_BLOCK // MIX_TM
MODS_TN = 1024

F32 = jnp.float32
BF16 = jnp.bfloat16


def _cparams(*sem):
    return pltpu.CompilerParams(dimension_semantics=sem, vmem_limit_bytes=V7X_VMEM_LIMIT_BYTES)


def _ln(x):
    mu = jnp.mean(x, axis=-1, keepdims=True)
    xc = x - mu
    var = jnp.mean(xc * xc, axis=-1, keepdims=True)
    return xc * lax.rsqrt(var + LN_EPS)


def _sigmoid(x):
    return 1.0 / (1.0 + jnp.exp(-x))


def _mods_kernel(c_ref, w_ref, b_ref, o_ref):
    c = c_ref[...]
    s = (c * _sigmoid(c)).astype(BF16)
    o_ref[0] = jnp.dot(s, w_ref[0].astype(BF16), preferred_element_type=F32) + b_ref[0]


def _mods(c8, ada_w, ada_b):
    n = 6 * D_MODEL
    return pl.pallas_call(
        _mods_kernel,
        grid=(DEPTH, n // MODS_TN),
        in_specs=[
            pl.BlockSpec((SUBLANES, D_MODEL), lambda l, j: (0, 0)),
            pl.BlockSpec((1, D_MODEL, MODS_TN), lambda l, j: (l, 0, j)),
            pl.BlockSpec((1, 1, MODS_TN), lambda l, j: (l, 0, j)),
        ],
        out_specs=pl.BlockSpec((1, SUBLANES, MODS_TN), lambda l, j: (l, 0, j)),
        out_shape=jax.ShapeDtypeStruct((DEPTH, SUBLANES, n), F32),
        compiler_params=_cparams("arbitrary", "arbitrary"),
        name="mods",
    )(c8, ada_w, ada_b.reshape(DEPTH, 1, n))


def _ln_proj_kernel(x_ref, m_ref, w_ref, o_ref, *, col0):
    m = m_ref[0]
    u = (_ln(x_ref[...]) * (1.0 + m[:, D_MODEL:2 * D_MODEL]) + m[:, :D_MODEL]).astype(BF16)
    for n in range(o_ref.shape[1] // PROJ_TN):
        w = w_ref[:, col0 + n * PROJ_TN:col0 + (n + 1) * PROJ_TN]
        o_ref[:, n * PROJ_TN:(n + 1) * PROJ_TN] = jnp.dot(
            u, w, preferred_element_type=F32).astype(o_ref.dtype)


def _ln_proj(x2d, mods3, w, col0, n_cols, mod_row_of_tile):
    n_rows = x2d.shape[0]
    return pl.pallas_call(
        functools.partial(_ln_proj_kernel, col0=col0),
        grid=(n_rows // PROJ_TM,),
        in_specs=[
            pl.BlockSpec((PROJ_TM, D_MODEL), lambda i: (i, 0)),
            pl.BlockSpec((1, 1, 6 * D_MODEL), lambda i: (mod_row_of_tile(i), 0, 0)),
            _resident(w.shape),
        ],
        out_specs=pl.BlockSpec((PROJ_TM, n_cols), lambda i: (i, 0)),
        out_shape=jax.ShapeDtypeStruct((n_rows, n_cols), BF16),
        compiler_params=_cparams("arbitrary"),
        name="ln_proj",
    )(x2d, mods3, w)


def _bias_table_kernel(rpb_ref, o_ref):
    h = pl.program_id(0)
    qc = lax.broadcasted_iota(jnp.int32, (GRID_W, GRID_W), 0)
    kc = lax.broadcasted_iota(jnp.int32, (GRID_W, GRID_W), 1)
    col_start = jnp.clip(qc - WIN_W // 2, 0, GRID_W - WIN_W)
    col_ok = (kc >= col_start) & (kc < col_start + WIN_W)
    col_idx = jnp.clip(kc - qc, -(WIN_W - 1), WIN_W - 1) + WIN_W - 1
    n_rb = 2 * WIN_H - 1
    n_cb = 2 * WIN_W - 1
    tiles = []
    for ri in range(n_rb):
        acc = jnp.zeros((GRID_W, GRID_W), F32)
        for i in range(n_cb):
            acc = jnp.where(col_idx == i, rpb_ref[(h * n_rb + ri) * n_cb + i], acc)
        tiles.append(jnp.where(col_ok, acc, NEG_INF))
    masked = jnp.full((GRID_W, GRID_W), NEG_INF, F32)
    for typ, blk in enumerate((0, 1, N_QBLOCKS - 1)):
        ks = min(max(blk * ATT_QROWS - WIN_H // 2, 0), GRID_ROWS - ATT_KROWS)
        for a in range(ATT_QROWS):
            r = blk * ATT_QROWS + a
            rs = min(max(r - WIN_H // 2, 0), GRID_ROWS - WIN_H)
            strip = [tiles[ks + c - r + WIN_H - 1] if rs <= ks + c < rs + WIN_H else masked
                     for c in range(ATT_KROWS)]
            o_ref[0, typ, a * GRID_W:(a + 1) * GRID_W, :] = jnp.concatenate(strip, axis=1)


def _bias_table(rpb_l):
    flat = rpb_l.reshape(-1).astype(F32)
    return pl.pallas_call(
        _bias_table_kernel,
        grid=(N_HEADS,),
        in_specs=[pl.BlockSpec(memory_space=pltpu.SMEM)],
        out_specs=pl.BlockSpec((1, 3, ATT_Q, ATT_K), lambda h: (h, 0, 0, 0)),
        out_shape=jax.ShapeDtypeStruct((N_HEADS, 3, ATT_Q, ATT_K), F32),
        compiler_params=_cparams("arbitrary"),
        name="bias_table",
    )(flat)


def _natten_kernel(q_ref, k_ref, v_ref, kc_ref, vc_ref, bias_ref, o_ref):
    scale = HEAD_DIM ** -0.5
    kc = kc_ref[...]
    vc = vc_ref[...]
    nt = (((1,), (1,)), ((), ()))

    def body(j, carry):
        typ = jnp.where(j == 0, 0, jnp.where(j == N_QBLOCKS - 1, 2, 1))
        q0 = pl.multiple_of(j * ATT_Q, ATT_Q)
        k0 = pl.multiple_of(
            jnp.clip(j * ATT_QROWS - WIN_H // 2, 0, GRID_ROWS - ATT_KROWS) * GRID_W, 4 * GRID_W)
        q = q_ref[pl.ds(q0, ATT_Q), :]
        kw = k_ref[pl.ds(k0, ATT_K), :]
        vw = v_ref[pl.ds(k0, ATT_K), :]
        s_loc = lax.dot_general(q, kw, nt, preferred_element_type=F32) * scale + bias_ref[typ]
        s_ctx = lax.dot_general(q, kc, nt, preferred_element_type=F32) * scale
        m = jnp.maximum(jnp.max(s_loc, axis=-1, keepdims=True), jnp.max(s_ctx, axis=-1, keepdims=True))
        p_loc = jnp.exp(s_loc - m)
        p_ctx = jnp.exp(s_ctx - m)
        denom = jnp.sum(p_loc, axis=-1, keepdims=True) + jnp.sum(p_ctx, axis=-1, keepdims=True)
        o = (jnp.dot(p_loc.astype(BF16), vw, preferred_element_type=F32)
             + jnp.dot(p_ctx.astype(BF16), vc, preferred_element_type=F32))
        o_ref[pl.ds(q0, ATT_Q), :] = (o / denom).astype(o_ref.dtype)
        return carry

    lax.fori_loop(0, N_QBLOCKS, body, 0)


def _natten(proj, kvc, bias):
    blk = lambda off: pl.BlockSpec((SEQ, HEAD_DIM), lambda b, h: (b, off + h))
    cblk = lambda off: pl.BlockSpec((CTX_LEN, HEAD_DIM), lambda b, h: (b, off + h))
    return pl.pallas_call(
        _natten_kernel,
        grid=(BATCH, N_HEADS),
        in_specs=[blk(0), blk(N_HEADS), blk(2 * N_HEADS), cblk(0), cblk(N_HEADS),
                  pl.BlockSpec((None, 3, ATT_Q, ATT_K), lambda b, h: (h, 0, 0, 0))],
        out_specs=pl.BlockSpec((SEQ, HEAD_DIM), lambda b, h: (b, h)),
        out_shape=jax.ShapeDtypeStruct((N_TOK, ATTN_WIDTH), BF16),
        compiler_params=_cparams("arbitrary", "arbitrary"),
        name="natten",
    )(proj, proj, proj, kvc, kvc, bias)


def _pack_bf16_pair(lo, hi):
    return pltpu.pack_elementwise([lo, hi], packed_dtype=BF16)


def _unpack_bf16_pair(w):
    unpack = lambda i: pltpu.unpack_elementwise(w, index=i, packed_dtype=BF16, unpacked_dtype=F32)
    return unpack(0), unpack(1)


def _store_token_tiles(ref, vals):
    tm, n = vals.shape[0], vals.shape[1] // LANES
    for c in range(n):
        ref[pl.ds(c, tm, stride=n), :] = vals[:, c * LANES:(c + 1) * LANES]


def _load_token_tiles(ref, tm):
    n = ref.shape[0] // tm
    return jnp.concatenate([ref[pl.ds(c, tm, stride=n), :] for c in range(n)], axis=1)


def _epilogue(y, s, x_ref, m_ref, lng_ref, lnb_ref, rwh_ref, rwl_ref, rb_ref,
              x1_ref, u2_ref, re_ref, rw_ref, rk_ref, cnt_ref):
    tm = MIX_TM
    rows = slice(s * tm, (s + 1) * tm)
    m = m_ref[0]
    g1 = m[:, 2 * D_MODEL:3 * D_MODEL]
    sh2 = m[:, 3 * D_MODEL:4 * D_MODEL]
    sc2 = m[:, 4 * D_MODEL:5 * D_MODEL]
    x1 = _ln(ALPHA * x_ref[rows, :] + g1 * y) * lng_ref[0:1, :] + lnb_ref[0:1, :]
    x1_ref[rows, :] = x1
    u2 = _ln(x1) * (1.0 + sc2) + sh2
    _store_token_tiles(u2_ref.at[pl.ds(s * tm * ROW_CHUNKS, tm * ROW_CHUNKS)],
                       _pack_bf16_pair(u2[:, :D_MODEL // 2], u2[:, D_MODEL // 2:]))

    u_hi = u2.astype(BF16)
    u_lo = (u2 - u_hi.astype(F32)).astype(BF16)
    logits = (jnp.dot(u_hi, rwh_ref[...], preferred_element_type=F32)
              + jnp.dot(u_lo, rwh_ref[...], preferred_element_type=F32)
              + jnp.dot(u_hi, rwl_ref[...], preferred_element_type=F32))
    aff = _sigmoid(logits.T[:N_EXPERTS, :])
    biased = aff + rb_ref[...]
    a = [biased[l * N_GROUPS:(l + 1) * N_GROUPS, :] for l in range(EXPERTS_PER_GROUP)]
    f = [aff[l * N_GROUPS:(l + 1) * N_GROUPS, :] for l in range(EXPERTS_PER_GROUP)]
    hi01, lo01 = jnp.maximum(a[0], a[1]), jnp.minimum(a[0], a[1])
    hi23, lo23 = jnp.maximum(a[2], a[3]), jnp.minimum(a[2], a[3])
    top1 = jnp.maximum(hi01, hi23)
    top2 = jnp.maximum(jnp.minimum(hi01, hi23), jnp.maximum(lo01, lo23))
    gscore = top1 + top2
    gi = lax.broadcasted_iota(jnp.int32, (N_GROUPS, tm), 0)
    gmax = jnp.max(gscore, axis=0, keepdims=True)
    group = jnp.min(jnp.where(gscore == gmax, gi, N_GROUPS), axis=0, keepdims=True)
    sel = gi == group
    b = [jnp.sum(jnp.where(sel, a[l], 0.0), axis=0, keepdims=True) for l in range(EXPERTS_PER_GROUP)]
    c = [jnp.sum(jnp.where(sel, f[l], 0.0), axis=0, keepdims=True) for l in range(EXPERTS_PER_GROUP)]

    def first_argmax(vals):
        mx = functools.reduce(jnp.maximum, vals)
        idx = jnp.full(mx.shape, len(vals) - 1, jnp.int32)
        for l in range(len(vals) - 2, -1, -1):
            idx = jnp.where(vals[l] == mx, l, idx)
        return idx

    i1 = first_argmax(b)
    i2 = first_argmax([jnp.where(i1 == l, -jnp.inf, b[l]) for l in range(EXPERTS_PER_GROUP)])
    pick = lambda idx: functools.reduce(
        lambda acc, l: jnp.where(idx == l, c[l], acc), range(1, EXPERTS_PER_GROUP), c[0])
    w1, w2 = pick(i1), pick(i2)
    wsum = w1 + w2
    e1 = group * EXPERTS_PER_GROUP + i1
    e2 = group * EXPERTS_PER_GROUP + i2
    re_ref[0:1, rows] = e1
    re_ref[1:2, rows] = e2
    rw_ref[0:1, rows] = w1 / wsum
    rw_ref[1:2, rows] = w2 / wsum

    ei = lax.broadcasted_iota(jnp.int32, (N_EXPERTS, tm), 0)
    oh1 = ei == e1
    oh2 = ei == e2
    oh = jnp.where(oh1 | oh2, 1.0, 0.0)
    t_row = lax.broadcasted_iota(jnp.int32, (tm, tm), 0)
    t_col = lax.broadcasted_iota(jnp.int32, (tm, tm), 1)
    before = jnp.where(t_row < t_col, 1.0, 0.0).astype(BF16)
    prefix = jnp.dot(oh.astype(BF16), before, preferred_element_type=F32)
    rk_ref[0:1, rows] = jnp.sum(jnp.where(oh1, prefix, 0.0), axis=0, keepdims=True).astype(jnp.int32)
    rk_ref[1:2, rows] = jnp.sum(jnp.where(oh2, prefix, 0.0), axis=0, keepdims=True).astype(jnp.int32)
    cnt = jnp.sum(oh, axis=1, keepdims=True)
    cnt_ref[s] = jnp.broadcast_to(cnt, (N_EXPERTS, LANES)).astype(jnp.int32)


def _resident(shape):
    return pl.BlockSpec(shape, lambda i: (0,) * len(shape), pipeline_mode=pl.Buffered(1))


def _epilogue_specs():
    blocks_per_seq = SEQ // MIX_BLOCK
    n_tiles = N_TOK // MIX_TM
    in_specs = [
        pl.BlockSpec((MIX_BLOCK, D_MODEL), lambda i: (i, 0)),
        pl.BlockSpec((1, 1, 6 * D_MODEL), lambda i: (i // blocks_per_seq, 0, 0)),
        _resident((2, D_MODEL)),
        _resident((2, D_MODEL)),
        _resident((D_MODEL, LANES)),
        _resident((D_MODEL, LANES)),
        _resident((N_EXPERTS, 1)),
    ]
    out_specs = [
        pl.BlockSpec((MIX_BLOCK, D_MODEL), lambda i: (i, 0)),
        pl.BlockSpec((MIX_BLOCK * ROW_CHUNKS, LANES), lambda i: (i, 0)),
        pl.BlockSpec((TOP_K, MIX_BLOCK), lambda i: (0, i)),
        pl.BlockSpec((TOP_K, MIX_BLOCK), lambda i: (0, i)),
        pl.BlockSpec((TOP_K, MIX_BLOCK), lambda i: (0, i)),
        pl.BlockSpec((MIX_SUBTILES, N_EXPERTS, LANES), lambda i: (i, 0, 0)),
    ]
    out_shape = [
        jax.ShapeDtypeStruct((N_TOK, D_MODEL), F32),
        jax.ShapeDtypeStruct((N_TOK * ROW_CHUNKS, LANES), jnp.uint32),
        jax.ShapeDtypeStruct((TOP_K, N_TOK), jnp.int32),
        jax.ShapeDtypeStruct((TOP_K, N_TOK), F32),
        jax.ShapeDtypeStruct((TOP_K, N_TOK), jnp.int32),
        jax.ShapeDtypeStruct((n_tiles, N_EXPERTS, LANES), jnp.int32),
    ]
    return in_specs, out_specs, out_shape


def _mix_out_kernel(attn_ref, gb_ref, gc_ref, xin_ref, gcp_ref, xinp_ref, gcn_ref, xinn_ref,
                    cw_ref, wo_ref, *epilogue_refs):
    i = pl.program_id(0)
    blocks_per_seq = SEQ // MIX_BLOCK
    z = gc_ref[...].astype(F32) * xin_ref[...].astype(F32)
    halo = 2 * SUBLANES
    zp_row = (gcp_ref[...].astype(F32) * xinp_ref[...].astype(F32))[halo - 1:halo, :]
    zn_row = (gcn_ref[...].astype(F32) * xinn_ref[...].astype(F32))[0:1, :]
    zp_row = jnp.where(i % blocks_per_seq == 0, 0.0, zp_row)
    zn_row = jnp.where(i % blocks_per_seq == blocks_per_seq - 1, 0.0, zn_row)
    row = lax.broadcasted_iota(jnp.int32, (MIX_BLOCK, 1), 0)
    z_prev = jnp.where(row == 0, zp_row, pltpu.roll(z, 1, axis=0))
    z_next = jnp.where(row == MIX_BLOCK - 1, zn_row, pltpu.roll(z, MIX_BLOCK - 1, axis=0))
    conv = cw_ref[0:1, :] * z_prev + cw_ref[1:2, :] * z + cw_ref[2:3, :] * z_next
    gated = (gb_ref[...].astype(F32) * conv).astype(BF16)
    y = (jnp.dot(attn_ref[...], wo_ref[:ATTN_WIDTH, :], preferred_element_type=F32)
         + jnp.dot(gated, wo_ref[ATTN_WIDTH:, :], preferred_element_type=F32))
    for s in range(MIX_SUBTILES):
        _epilogue(y[s * MIX_TM:(s + 1) * MIX_TM, :], s, *epilogue_refs)


def _mix_out(attn, proj, conv_w, w_out, x2d, mods3, ln_g, ln_b, rw_hi, rw_lo, rb):
    halo = 2 * SUBLANES
    hb = MIX_BLOCK // halo
    n_halo_blocks = N_TOK // halo
    cblk = lambda off: pl.BlockSpec((MIX_BLOCK, CONV_CH), lambda i: (i, off))
    prev = lambda off: pl.BlockSpec((halo, CONV_CH), lambda i: (jnp.maximum(i * hb - 1, 0), off))
    nxt = lambda off: pl.BlockSpec(
        (halo, CONV_CH), lambda i: (jnp.minimum((i + 1) * hb, n_halo_blocks - 1), off))
    ep_in, out_specs, out_shape = _epilogue_specs()
    return pl.pallas_call(
        _mix_out_kernel,
        grid=(N_TOK // MIX_BLOCK,),
        in_specs=[pl.BlockSpec((MIX_BLOCK, ATTN_WIDTH), lambda i: (i, 0)),
                  cblk(3), cblk(4), cblk(5), prev(4), prev(5), nxt(4), nxt(5),
                  _resident((3, CONV_CH)), _resident((D_MODEL, D_MODEL))] + ep_in,
        out_specs=out_specs,
        out_shape=out_shape,
        compiler_params=_cparams("arbitrary"),
        name="mix_out",
    )(attn, proj, proj, proj, proj, proj, proj, proj, conv_w, w_out, x2d, mods3, ln_g, ln_b,
      rw_hi, rw_lo, rb)


def _dft_seq_kernel(cs_ref, tw_ref, u0_ref, u1_ref, u2_ref, u3_ref, o_ref):
    c_m, s_m = cs_ref[0], cs_ref[1]
    tr, ti = [], []
    for n2, u_ref in enumerate((u0_ref, u1_ref, u2_ref, u3_ref)):
        x = u_ref[...]
        a = jnp.dot(c_m, x, preferred_element_type=F32)
        b = jnp.dot(s_m, x, preferred_element_type=F32)
        if n2 == 0:
            tr.append(a)
            ti.append(b)
        else:
            c = tw_ref[:, 2 * n2 - 2:2 * n2 - 1]
            s = tw_ref[:, 2 * n2 - 1:2 * n2]
            tr.append(a * c - b * s)
            ti.append(a * s + b * c)
    o_ref[0, 0] = (tr[0] + tr[1] + tr[2] + tr[3]).astype(o_ref.dtype)
    o_ref[1, 0] = (ti[0] + ti[1] + ti[2] + ti[3]).astype(o_ref.dtype)
    o_ref[0, 1] = (tr[0] - ti[1] - tr[2] + ti[3]).astype(o_ref.dtype)
    o_ref[1, 1] = (ti[0] + tr[1] - ti[2] - tr[3]).astype(o_ref.dtype)
    o_ref[0, 2] = (tr[0] - tr[1] + tr[2] - tr[3]).astype(o_ref.dtype)
    o_ref[1, 2] = (ti[0] - ti[1] + ti[2] - ti[3]).astype(o_ref.dtype)
    o_ref[0, 3] = (tr[0] + ti[1] - tr[2] - ti[3]).astype(o_ref.dtype)
    o_ref[1, 3] = (ti[0] - tr[1] - ti[2] + tr[3]).astype(o_ref.dtype)


def _dft_seq(cs, tw, u4):
    n_col = D_MODEL // DFT_TN
    sub = lambda n2: pl.BlockSpec((DFT_SUB, DFT_TN), lambda b, n, i: (b, n2 * n_col + n))
    out = pl.pallas_call(
        _dft_seq_kernel,
        grid=(BATCH, n_col, DFT_SUB // DFT_TK),
        in_specs=[pl.BlockSpec((2, DFT_TK, DFT_SUB), lambda b, n, i: (0, i, 0)),
                  pl.BlockSpec((DFT_TK, 2 * DFT_RADIX), lambda b, n, i: (i, 0)),
                  sub(0), sub(1), sub(2), sub(3)],
        out_specs=pl.BlockSpec((None, 2, DFT_RADIX, DFT_TK, DFT_TN), lambda b, n, i: (b, 0, 0, i, n)),
        out_shape=jax.ShapeDtypeStruct((BATCH, 2, DFT_RADIX, DFT_SUB, D_MODEL), BF16),
        compiler_params=_cparams("arbitrary", "arbitrary", "arbitrary"),
        name="dft_seq",
    )(cs, tw, u4, u4, u4, u4)
    return out.reshape(BATCH * 2 * SEQ, D_MODEL)


def _four_out_kernel(p_ref, q_ref, cc_ref, sc_ref, wf_ref, *epilogue_refs):
    y = jnp.zeros((MIX_BLOCK, D_MODEL), F32)
    for g in range(N_FOURIER_GROUPS):
        cols = slice(g * FOURIER_CH, (g + 1) * FOURIER_CH)
        fg = (jnp.dot(p_ref[:, cols], cc_ref[...], preferred_element_type=F32)
              - jnp.dot(q_ref[:, cols], sc_ref[...], preferred_element_type=F32))
        y = y + jnp.dot(fg.astype(BF16), wf_ref[cols, :], preferred_element_type=F32)
    for s in range(MIX_SUBTILES):
        _epilogue(y[s * MIX_TM:(s + 1) * MIX_TM, :], s, *epilogue_refs)


def _four_out(pq, cc, sc, wf, x2d, mods3, ln_g, ln_b, rw_hi, rw_lo, rb):
    blocks_per_seq = SEQ // MIX_BLOCK
    ep_in, out_specs, out_shape = _epilogue_specs()
    prow = lambda i: (i // blocks_per_seq) * 2 * blocks_per_seq + i % blocks_per_seq
    return pl.pallas_call(
        _four_out_kernel,
        grid=(N_TOK // MIX_BLOCK,),
        in_specs=[pl.BlockSpec((MIX_BLOCK, D_MODEL), lambda i: (prow(i), 0)),
                  pl.BlockSpec((MIX_BLOCK, D_MODEL), lambda i: (prow(i) + blocks_per_seq, 0)),
                  _resident((FOURIER_CH, FOURIER_CH)), _resident((FOURIER_CH, FOURIER_CH)),
                  _resident((D_MODEL, D_MODEL))] + ep_in,
        out_specs=out_specs,
        out_shape=out_shape,
        compiler_params=_cparams("arbitrary"),
        name="four_out",
    )(pq, pq, cc, sc, wf, x2d, mods3, ln_g, ln_b, rw_hi, rw_lo, rb)


def _dispatch_kernel(dest, fill_start, fill_n, u_ref, out_hbm, sem):
    i = pl.program_id(0)
    base = i * DISPATCH_TM

    def row_dma(r, d):
        src = u_ref.at[pl.ds(pl.multiple_of(r * ROW_CHUNKS, ROW_CHUNKS), ROW_CHUNKS)]
        dst = out_hbm.at[pl.ds(pl.multiple_of(d * ROW_CHUNKS, ROW_CHUNKS), ROW_CHUNKS)]
        return pltpu.make_async_copy(src, dst, sem)

    def issue(r, c):
        for k in range(TOP_K):
            row_dma(r, dest[k * N_TOK + base + r]).start(priority=k)
        return c

    def drain(r, c):
        for k in range(TOP_K):
            row_dma(0, 0).wait()
        return c

    lax.fori_loop(0, DISPATCH_TM, issue, 0, unroll=8)
    lax.fori_loop(0, DISPATCH_TM, drain, 0, unroll=8)

    @pl.when(i == pl.num_programs(0) - 1)
    def _():
        def per_segment(e, c):
            lax.fori_loop(0, fill_n[e], lambda r, c2: (row_dma(0, fill_start[e] + r).start(), c2)[1], 0)
            lax.fori_loop(0, fill_n[e], lambda r, c2: (row_dma(0, 0).wait(), c2)[1], 0)
            return c

        lax.fori_loop(0, N_FILL_SEGMENTS, per_segment, 0)


def _dispatch(u2p, dest, fill_start, fill_n):
    return pl.pallas_call(
        _dispatch_kernel,
        grid_spec=pltpu.PrefetchScalarGridSpec(
            num_scalar_prefetch=3,
            grid=(N_TOK // DISPATCH_TM,),
            in_specs=[pl.BlockSpec((DISPATCH_TM * ROW_CHUNKS, LANES), lambda i, *_: (i, 0))],
            out_specs=pl.BlockSpec(memory_space=pl.ANY),
            scratch_shapes=[pltpu.SemaphoreType.DMA(())],
        ),
        out_shape=jax.ShapeDtypeStruct((N_PAD * ROW_CHUNKS, LANES), jnp.uint32),
        compiler_params=_cparams("arbitrary"),
        name="dispatch",
    )(dest, fill_start, fill_n, u2p)


def _expert_weights(layer, be_ref, nxt_ref, w_hbm, stage, sem, w_bf16):
    i = pl.program_id(0)
    changed = (i == 0) | (be_ref[i] != be_ref[jnp.maximum(i - 1, 0)])

    def copies(e):
        out = []
        for w, st in zip(w_hbm, stage):
            rows = st.shape[0] // WEIGHT_DMA_CHUNKS
            for c in range(WEIGHT_DMA_CHUNKS):
                rs = pl.ds(c * rows, rows)
                out.append(pltpu.make_async_copy(w.at[layer, e, rs], st.at[rs], sem))
        return out

    @pl.when(changed)
    def _():
        @pl.when(i == 0)
        def _():
            for cp in copies(be_ref[0]):
                cp.start(priority=1)

        for cp in copies(be_ref[i]):
            cp.wait()
        for st, wb in zip(stage, w_bf16):
            def cast_rows(r, c, st=st, wb=wb):
                rs = pl.ds(pl.multiple_of(r * CAST_ROWS, CAST_ROWS), CAST_ROWS)
                wb[rs, :] = st[rs, :].astype(BF16)
                return c
            lax.fori_loop(0, st.shape[0] // CAST_ROWS, cast_rows, 0)

        @pl.when(nxt_ref[i] >= 0)
        def _():
            for cp in copies(nxt_ref[i]):
                cp.start(priority=1)


def _experts_kernel(be_ref, nv_ref, nxt_ref, x_ref, wg_hbm, wu_hbm, wd_hbm, o_ref,
                    wg_st, wu_st, wd_st, sem, wg_b, wu_b, wd_b, *, layer):
    _expert_weights(layer, be_ref, nxt_ref, (wg_hbm, wu_hbm, wd_hbm), (wg_st, wu_st, wd_st), sem,
                    (wg_b, wu_b, wd_b))

    @pl.when(pl.program_id(0) >= nv_ref[0])
    def _():
        o_ref[...] = jnp.zeros_like(o_ref)

    @pl.when(pl.program_id(0) < nv_ref[0])
    def _():
        lo, hi = _unpack_bf16_pair(_load_token_tiles(x_ref, EXPERT_TM))
        lo, hi = lo.astype(BF16), hi.astype(BF16)
        half = D_MODEL // 2
        gate = (jnp.dot(lo, wg_b[:half, :], preferred_element_type=F32)
                + jnp.dot(hi, wg_b[half:, :], preferred_element_type=F32))
        up = (jnp.dot(lo, wu_b[:half, :], preferred_element_type=F32)
              + jnp.dot(hi, wu_b[half:, :], preferred_element_type=F32))
        act = (gate * _sigmoid(gate) * up).astype(BF16)
        _store_token_tiles(o_ref, jnp.dot(act, wd_b[...], preferred_element_type=F32))


def _experts(layer, sched, xs, w_gate, w_up, w_down):
    any_spec = pl.BlockSpec(memory_space=pl.ANY)
    up_shape, down_shape = (D_MODEL, D_EXPERT), (D_EXPERT, D_MODEL)
    return pl.pallas_call(
        functools.partial(_experts_kernel, layer=layer),
        grid_spec=pltpu.PrefetchScalarGridSpec(
            num_scalar_prefetch=3,
            grid=(N_EXPERT_BLOCKS,),
            in_specs=[pl.BlockSpec((EXPERT_TM * ROW_CHUNKS, LANES), lambda i, *_: (i, 0)),
                      any_spec, any_spec, any_spec],
            out_specs=pl.BlockSpec((EXPERT_TM * OUT_CHUNKS, LANES), lambda i, *_: (i, 0)),
            scratch_shapes=[pltpu.VMEM(up_shape, F32), pltpu.VMEM(up_shape, F32),
                            pltpu.VMEM(down_shape, F32), pltpu.SemaphoreType.DMA(()),
                            pltpu.VMEM(up_shape, BF16), pltpu.VMEM(up_shape, BF16),
                            pltpu.VMEM(down_shape, BF16)],
        ),
        out_shape=jax.ShapeDtypeStruct((N_PAD * OUT_CHUNKS, LANES), F32),
        compiler_params=_cparams("arbitrary"),
        name="experts",
    )(*sched, xs, w_gate, w_up, w_down)


def _final_kernel(dest, ys_hbm, rw_ref, x1_ref, m_ref, lng_ref, lnb_ref, *rest, emit_next):
    if emit_next:
        mn_ref, x2_ref, un_ref, buf, sem, un_scr = rest
    else:
        x2_ref, buf, sem = rest
    i = pl.program_id(0)
    n = pl.num_programs(0)

    def row_dma(tile, slot, r, k):
        d = dest[k * N_TOK + tile * MIX_TM + r]
        src = ys_hbm.at[pl.ds(pl.multiple_of(d * OUT_CHUNKS, OUT_CHUNKS), OUT_CHUNKS)]
        dst = buf.at[slot, k, pl.ds(pl.multiple_of(r * OUT_CHUNKS, OUT_CHUNKS), OUT_CHUNKS)]
        return pltpu.make_async_copy(src, dst, sem.at[slot])

    def gather(tile, slot):
        def issue(r, c):
            for k in range(TOP_K):
                row_dma(tile, slot, r, k).start(priority=k)
            return c
        lax.fori_loop(0, MIX_TM, issue, 0, unroll=8)

    @pl.when(i == 0)
    def _():
        gather(0, 0)

    @pl.when(i + 1 < n)
    def _():
        gather(i + 1, (i + 1) % 2)

    slot = i % 2

    def drain(r, c):
        for k in range(TOP_K):
            row_dma(0, slot, 0, k).wait()
        return c
    lax.fori_loop(0, MIX_TM, drain, 0, unroll=8)

    expert_rows = lambda k: _load_token_tiles(buf.at[slot, k], MIX_TM)
    m = m_ref[0]
    g2 = m[:, 5 * D_MODEL:6 * D_MODEL]
    f = rw_ref[:, 0:1] * expert_rows(0) + rw_ref[:, 1:2] * expert_rows(1)
    x2 = _ln(ALPHA * x1_ref[...] + g2 * f) * lng_ref[1:2, :] + lnb_ref[1:2, :]
    x2_ref[...] = x2
    if emit_next:
        mn = mn_ref[0]
        un = _ln(x2) * (1.0 + mn[:, D_MODEL:2 * D_MODEL]) + mn[:, :D_MODEL]
        for c in range(OUT_CHUNKS):
            un_scr[c] = un[:, c * LANES:(c + 1) * LANES]
        for n2 in range(DFT_RADIX):
            rows = pl.ds(n2, MIX_TM // DFT_RADIX, stride=DFT_RADIX)
            un_ref[:, n2 * D_MODEL:(n2 + 1) * D_MODEL] = jnp.concatenate(
                [un_scr[c, rows, :] for c in range(OUT_CHUNKS)], axis=1).astype(BF16)


def _final(dest, ys, rw_t, x1, mods3, ln_g, ln_b, mods3_next=None):
    tiles_per_seq = SEQ // MIX_TM
    emit_next = mods3_next is not None
    mspec = pl.BlockSpec((1, 1, 6 * D_MODEL), lambda i, d: (i // tiles_per_seq, 0, 0))
    row = pl.BlockSpec((MIX_TM, D_MODEL), lambda i, d: (i, 0))
    in_specs = [pl.BlockSpec(memory_space=pl.ANY),
                pl.BlockSpec((MIX_TM, TOP_K), lambda i, d: (i, 0)),
                row, mspec,
                pl.BlockSpec((2, D_MODEL), lambda i, d: (0, 0)),
                pl.BlockSpec((2, D_MODEL), lambda i, d: (0, 0))]
    args = [ys, rw_t, x1, mods3, ln_g, ln_b]
    out_specs = [row]
    out_shape = [jax.ShapeDtypeStruct((N_TOK, D_MODEL), F32)]
    scratch = [pltpu.VMEM((2, TOP_K, MIX_TM * OUT_CHUNKS, LANES), F32), pltpu.SemaphoreType.DMA((2,))]
    if emit_next:
        in_specs.append(mspec)
        args.append(mods3_next)
        out_specs.append(pl.BlockSpec((MIX_TM // DFT_RADIX, DFT_RADIX * D_MODEL), lambda i, d: (i, 0)))
        out_shape.append(jax.ShapeDtypeStruct((N_TOK // DFT_RADIX, DFT_RADIX * D_MODEL), BF16))
        scratch.append(pltpu.VMEM((OUT_CHUNKS, MIX_TM, LANES), F32))
    return pl.pallas_call(
        functools.partial(_final_kernel, emit_next=emit_next),
        grid_spec=pltpu.PrefetchScalarGridSpec(
            num_scalar_prefetch=1,
            grid=(N_TOK // MIX_TM,),
            in_specs=in_specs,
            out_specs=out_specs,
            scratch_shapes=scratch,
        ),
        out_shape=out_shape,
        compiler_params=_cparams("arbitrary"),
        name="final",
    )(dest, *args)


def _moe(layer, u2p, route_e, route_w, rank, cnt, w_gate, w_up, w_down):
    n_tiles = N_TOK // MIX_TM
    cnt = cnt[:, :, 0]
    counts = jnp.sum(cnt, axis=0)
    padded = (counts + EXPERT_TM - 1) // EXPERT_TM * EXPERT_TM
    pad_end = jnp.cumsum(padded)
    pad_start = pad_end - padded
    base = pad_start[None, :] + jnp.cumsum(cnt, axis=0) - cnt
    e_r = route_e.reshape(TOP_K, n_tiles, MIX_TM)
    onehot = e_r[..., None] == jnp.arange(N_EXPERTS, dtype=jnp.int32)
    dest = jnp.sum(jnp.where(onehot, base[None, :, None, :], 0), axis=-1).reshape(TOP_K, N_TOK) + rank
    dest = dest.reshape(-1).astype(jnp.int32)
    n_valid = (pad_end[-1] // EXPERT_TM).astype(jnp.int32).reshape(1)
    block_row0 = jnp.arange(N_EXPERT_BLOCKS, dtype=jnp.int32) * EXPERT_TM
    present = padded > 0
    expert_ids = jnp.arange(N_EXPERTS, dtype=jnp.int32)
    last_expert = jnp.max(jnp.where(present, expert_ids, 0))
    block_expert = jnp.minimum(
        jnp.sum(pad_end[None, :] <= block_row0[:, None], axis=1), last_expert).astype(jnp.int32)
    later = present[None, :] & (expert_ids[None, :] > block_expert[:, None])
    block_next = jnp.min(jnp.where(later, expert_ids[None, :], N_EXPERTS), axis=1)
    block_next = jnp.where(block_next == N_EXPERTS, -1, block_next)
    sched = (block_expert, n_valid, block_next.astype(jnp.int32))

    fill_start = jnp.concatenate([pad_start + counts, pad_end[-1:]]).astype(jnp.int32)
    fill_n = jnp.concatenate([padded - counts, N_PAD - pad_end[-1:]]).astype(jnp.int32)
    xs = _dispatch(u2p, dest, fill_start, fill_n)
    ys = _experts(layer, sched, xs, w_gate, w_up, w_down)
    return dest, ys, route_w.T


def _dft_tables():
    dft = lambda n: 2 * np.pi * (np.outer(np.arange(n), np.arange(n)) % n) / n
    ang = dft(DFT_SUB)
    cs = jnp.asarray(np.stack([np.cos(ang), np.sin(ang)]) * SEQ ** -0.5, F32).astype(BF16)
    k1 = np.arange(DFT_SUB, dtype=np.float64)
    tw = np.zeros((DFT_SUB, 2 * DFT_RADIX), np.float64)
    for n2 in range(1, DFT_RADIX):
        tw[:, 2 * n2 - 2] = np.cos(2 * np.pi * n2 * k1 / SEQ)
        tw[:, 2 * n2 - 1] = np.sin(2 * np.pi * n2 * k1 / SEQ)
    ang = dft(FOURIER_CH)
    cc = jnp.asarray(np.cos(ang) * FOURIER_CH ** -0.5, F32).astype(BF16)
    sc = jnp.asarray(np.sin(ang) * FOURIER_CH ** -0.5, F32).astype(BF16)
    return cs, jnp.asarray(tw, F32), cc, sc


def kernel(x, c, ctx, c_ctx, ada_w, ada_b, w_mix_in, rpb, conv_w, w_mix_out, w_fourier_out,
           router_w, router_b, w_gate, w_up, w_down, ln_g, ln_b):
    x2d = x.reshape(N_TOK, D_MODEL)
    c8 = jnp.concatenate(
        [c, c_ctx[None, :], jnp.zeros((SUBLANES - BATCH - 1, D_MODEL), F32)], axis=0)
    mods = _mods(c8, ada_w, ada_b)
    mods3 = [mods[l].reshape(SUBLANES, 1, 6 * D_MODEL) for l in range(DEPTH)]
    perm = lambda a: a.reshape(N_GROUPS, EXPERTS_PER_GROUP, -1).transpose(1, 0, 2).reshape(N_EXPERTS, -1)
    rw = jnp.pad(perm(router_w.T).T, ((0, 0), (0, LANES - N_EXPERTS)))
    rw_hi = rw.astype(BF16)
    rw_lo = (rw - rw_hi.astype(F32)).astype(BF16)
    rb = perm(router_b.reshape(N_EXPERTS, 1))
    tiles_per_seq = SEQ // PROJ_TM

    w_in = w_mix_in[0].astype(BF16)
    proj = _ln_proj(x2d, mods3[0], w_in, 0, MIX_IN_WIDTH, lambda i: i // tiles_per_seq)
    kvc = _ln_proj(ctx.reshape(BATCH * CTX_LEN, D_MODEL), mods3[0], w_in,
                   ATTN_WIDTH, 2 * ATTN_WIDTH, lambda i: BATCH)
    attn = _natten(proj, kvc, _bias_table(rpb[0]))
    x1, u2p, route_e, route_w, rank, cnt = _mix_out(
        attn, proj, conv_w[0], w_mix_out[0].astype(BF16), x2d, mods3[0], ln_g[0], ln_b[0],
        rw_hi, rw_lo, rb)
    dest, ys, rw_t = _moe(0, u2p, route_e, route_w, rank, cnt, w_gate, w_up, w_down)
    x2, u_next = _final(dest, ys, rw_t, x1, mods3[0], ln_g[0], ln_b[0], mods3_next=mods3[1])

    cs, tw, cc, sc = _dft_tables()
    pq = _dft_seq(cs, tw, u_next)
    x1, u2p, route_e, route_w, rank, cnt = _four_out(
        pq, cc, sc, w_fourier_out[0].astype(BF16), x2, mods3[1], ln_g[1], ln_b[1],
        rw_hi, rw_lo, rb)
    dest, ys, rw_t = _moe(1, u2p, route_e, route_w, rank, cnt, w_gate, w_up, w_down)
    (x3,) = _final(dest, ys, rw_t, x1, mods3[1], ln_g[1], ln_b[1])
    return x3.reshape(BATCH, SEQ, D_MODEL)
```
